```python
import jax, jax.numpy as jnp
from jax import lax
import numpy as np

D_MODEL = 2048
BATCH = 2
SEQ = 4096
DEPTH = 2

HEAD_DIM = 64
A_HEADS = 8
A_WIDTH = A_HEADS * HEAD_DIM
CHUNK = 128
B_WIDTH = 768
CONV_WIDTH = 3
DILATION_PATTERNS = ((128, 1), (512, 4), (2048, 16))
C_HEADS_PER_PATTERN = 4
C_HEADS = C_HEADS_PER_PATTERN * len(DILATION_PATTERNS)
C_WIDTH = C_HEADS * HEAD_DIM
D_MIX = A_WIDTH + B_WIDTH + C_WIDTH
PROJ_SIZES = (A_WIDTH, A_WIDTH, B_WIDTH, B_WIDTH, B_WIDTH, C_WIDTH, C_WIDTH, C_WIDTH)
PROJ_SPLITS = tuple(int(s) for s in np.cumsum(PROJ_SIZES)[:-1])
D_IN_PROJ = sum(PROJ_SIZES)
D_FF = 4 * D_MODEL
EPS = 1e-6

kernel_name = 'hymba_style_sgu_shortconv_dilated_attn'


def rms_norm(x, g):
    x32 = x.astype(jnp.float32)
    y = x32 * lax.rsqrt(jnp.mean(x32 * x32, axis=-1, keepdims=True) + EPS)
    return (y * g.astype(jnp.float32)).astype(x.dtype)


def spatial_gating(u, v, w_s, b_s):
    bsz, s = u.shape[:2]
    n_chunks = s // CHUNK
    vc = v.reshape(bsz, n_chunks, CHUNK, A_HEADS, HEAD_DIM)
    w_causal = jnp.tril(w_s)
    mixed = jnp.einsum('hij,bcjhd->bcihd', w_causal, vc) + b_s.T[None, None, :, :, None]
    return u * mixed.reshape(bsz, s, A_WIDTH)


def short_gated_conv(b_gate, c_gate, xb, w_conv):
    z = c_gate * xb
    zp = jnp.pad(z, ((0, 0), (CONV_WIDTH - 1, 0), (0, 0)))
    s = z.shape[1]
    conv = sum(w_conv[i] * zp[:, i:i + s] for i in range(CONV_WIDTH))
    return b_gate * conv


def head_rms_norm(x, g):
    x32 = x.astype(jnp.float32)
    return x32 * lax.rsqrt(jnp.mean(x32 * x32, axis=-1, keepdims=True) + EPS) * g.astype(jnp.float32)


def dilated_window_attention(q, k, v, window, dilation):
    bsz, s, h, d = q.shape
    length = s // dilation
    blk = window // dilation
    nb = -(-length // blk)
    lp = nb * blk

    def to_sub(t):
        t = t.reshape(bsz, length, dilation, h, d).transpose(0, 2, 1, 3, 4)
        t = jnp.pad(t, ((0, 0), (0, 0), (0, lp - length), (0, 0), (0, 0)))
        return t.reshape(bsz, dilation, nb, blk, h, d)

    qb = to_sub(q)
    kb = to_sub(k)
    vb = to_sub(v.astype(jnp.float32))
    pad_prev = ((0, 0), (0, 0), (1, 0), (0, 0), (0, 0), (0, 0))
    kcat = jnp.concatenate([jnp.pad(kb, pad_prev)[:, :, :-1], kb], axis=3)
    vcat = jnp.concatenate([jnp.pad(vb, pad_prev)[:, :, :-1], vb], axis=3)

    scores = jnp.einsum('brnqhd,brnkhd->brnhqk', qb, kcat) * (HEAD_DIM ** -0.5)
    qi = jnp.arange(blk)[:, None]
    kj = jnp.arange(2 * blk)[None, :]
    band = (kj >= qi) & (kj <= qi + blk)
    has_prev = jnp.arange(nb)[:, None, None] > 0
    mask = band[None] & (has_prev | (kj >= blk)[None])
    scores = jnp.where(mask[None, None, :, None], scores, -jnp.inf)
    m = jnp.max(scores, axis=-1, keepdims=True)
    e = jnp.exp(scores - m)
    den = jnp.sum(e, axis=-1, keepdims=True)
    o = jnp.einsum('brnhqk,brnkhd->brnhqd', e, vcat) / den
    lse = (m + jnp.log(den))[..., 0]

    o = o.transpose(0, 1, 2, 4, 3, 5).reshape(bsz, dilation, lp, h, d)[:, :, :length]
    o = o.transpose(0, 2, 1, 3, 4).reshape(bsz, s, h, d)
    lse = lse.transpose(0, 1, 2, 4, 3).reshape(bsz, dilation, lp, h)[:, :, :length]
    lse = lse.transpose(0, 2, 1, 3).reshape(bsz, s, h)
    return o, lse


def dilated_mixture(q, k, v, q_g, k_g):
    bsz, s = q.shape[:2]
    q = head_rms_norm(q.reshape(bsz, s, C_HEADS, HEAD_DIM), q_g)
    k = head_rms_norm(k.reshape(bsz, s, C_HEADS, HEAD_DIM), k_g)
    v = v.reshape(bsz, s, C_HEADS, HEAD_DIM)
    outs, lses = [], []
    for g, (window, dilation) in enumerate(DILATION_PATTERNS):
        sl = slice(g * C_HEADS_PER_PATTERN, (g + 1) * C_HEADS_PER_PATTERN)
        o, lse = dilated_window_attention(q[:, :, sl], k[:, :, sl], v[:, :, sl], window, dilation)
        outs.append(o)
        lses.append(lse)
    alpha = jax.nn.softmax(jnp.stack(lses, axis=0), axis=0)
    y = jnp.stack(outs, axis=0) * alpha[..., None]
    return y.transpose(1, 2, 0, 3, 4).reshape(bsz, s, C_WIDTH).astype(v.dtype)


def setup_inputs(seed: int = 0) -> dict:
    key = jax.random.key(seed)
    ks = jax.random.split(key, 12)
    n = jax.random.normal
    f32 = jnp.float32
    return {
        'x': n(ks[0], (BATCH, SEQ, D_MODEL), f32),
        'attn_norm': 1.0 + 0.02 * n(ks[1], (DEPTH, D_MODEL), f32),
        'w_in': n(ks[2], (DEPTH, D_MODEL, D_IN_PROJ), f32) * D_MODEL ** -0.5,
        'sgu_w': n(ks[3], (DEPTH, A_HEADS, CHUNK, CHUNK), f32) * CHUNK ** -0.5,
        'sgu_b': 1.0 + 0.1 * n(ks[4], (DEPTH, A_HEADS, CHUNK), f32),
        'conv_w': n(ks[5], (DEPTH, CONV_WIDTH, B_WIDTH), f32) * CONV_WIDTH ** -0.5,
        'q_norm': 1.0 + 0.02 * n(ks[6], (DEPTH, HEAD_DIM), f32),
        'k_norm': 1.0 + 0.02 * n(ks[7], (DEPTH, HEAD_DIM), f32),
        'w_out': n(ks[8], (DEPTH, D_MIX, D_MODEL), f32) * D_MIX ** -0.5,
        'mlp_norm': 1.0 + 0.02 * n(ks[9], (DEPTH, D_MODEL), f32),
        'w_mlp_in': n(ks[10], (DEPTH, D_MODEL, D_FF), f32) * D_MODEL ** -0.5,
        'w_mlp_out': n(ks[11], (DEPTH, D_FF, D_MODEL), f32) * D_FF ** -0.5,
    }


def reference(x, attn_norm, w_in, sgu_w, sgu_b, conv_w, q_norm, k_norm, w_out,
              mlp_norm, w_mlp_in, w_mlp_out):
    for l in range(DEPTH):
        h = rms_norm(x, attn_norm[l])
        p = h @ w_in[l]
        a_u, a_v, b_b, b_c, b_x, q, k, v = jnp.split(p, PROJ_SPLITS, axis=-1)
        y_a = spatial_gating(a_u, a_v, sgu_w[l], sgu_b[l])
        y_b = short_gated_conv(b_b, b_c, b_x, conv_w[l])
        y_c = dilated_mixture(q, k, v, q_norm[l], k_norm[l])
        x = x + jnp.concatenate([y_a, y_b, y_c], axis=-1) @ w_out[l]
        h = rms_norm(x, mlp_norm[l])
        x = x + jnp.square(jax.nn.relu(h @ w_mlp_in[l])) @ w_mlp_out[l]
    return x
```

```python
import functools

import jax
import jax.numpy as jnp
from jax import lax
from jax.experimental import pallas as pl
from jax.experimental.pallas import tpu as pltpu

F32 = jnp.float32
BF16 = jnp.bfloat16

D_MODEL = 2048
HEAD_DIM = 64
A_WIDTH = 512
B_WIDTH = 768
C_WIDTH = 768
CHUNK = 128
DILATIONS = (1, 4, 16)
ATT_BLK = 128
GROUP_W = 256
D_IN_PROJ = 5632
AB_WIDTH = 2 * A_WIDTH + 3 * B_WIDTH
Q_COL, K_COL, V_COL = 13, 16, 19
PROJ_GROUPS = D_IN_PROJ // GROUP_W
D_FF = 4 * D_MODEL
EPS = 1e-6
NEG_BIG = -1e30

V7X_VMEM_BYTES = 64 * 1024 * 1024


def _vmem_limit(estimate_bytes):
    return int(min(V7X_VMEM_BYTES - 4 * 1024 * 1024, estimate_bytes))


def _rms_norm_bf16(x, g):
    ms = jnp.mean(x * x, axis=-1, keepdims=True)
    return ((x * lax.rsqrt(ms + EPS)) * g).astype(BF16)


IN_TM = 512
IN_TN = 512


def _in_proj_body(x_ref, g_ref, w_ref, p_ref, h_ref):
    h_ref[...] = _rms_norm_bf16(x_ref[...], g_ref[...])
    for c in range(D_IN_PROJ // IN_TN):
        cols = slice(c * IN_TN, (c + 1) * IN_TN)
        p_ref[:, cols] = jnp.dot(h_ref[...], w_ref[:, cols],
                                 preferred_element_type=F32).astype(BF16)


def _in_proj(x, g, w):
    t = x.shape[0]
    est = (2 * IN_TM * D_MODEL * 4 + 2 * D_MODEL * D_IN_PROJ * 2 + 2 * IN_TM * D_IN_PROJ * 2
           + IN_TM * D_MODEL * 2 + 4 * IN_TM * IN_TN * 4)
    return pl.pallas_call(
        _in_proj_body,
        out_shape=jax.ShapeDtypeStruct((t, D_IN_PROJ), BF16),
        grid=(t // IN_TM,),
        in_specs=[
            pl.BlockSpec((IN_TM, D_MODEL), lambda i: (i, 0)),
            pl.BlockSpec((1, D_MODEL), lambda i: (0, 0)),
            pl.BlockSpec((D_MODEL, D_IN_PROJ), lambda i: (0, 0), pipeline_mode=pl.Buffered(1)),
        ],
        out_specs=pl.BlockSpec((IN_TM, D_IN_PROJ), lambda i: (i, 0)),
        scratch_shapes=[pltpu.VMEM((IN_TM, D_MODEL), BF16)],
        compiler_params=pltpu.CompilerParams(
            dimension_semantics=("arbitrary",), vmem_limit_bytes=_vmem_limit(est)),
        name="in_proj",
    )(x, g, w)


ATT_QROWS = 2 * ATT_BLK


def _head_lane_masks():
    lane = lax.broadcasted_iota(jnp.int32, (1, GROUP_W), 1)
    return [(lane >= h * HEAD_DIM) & (lane < (h + 1) * HEAD_DIM) for h in range(4)]


def _head_norm(t_bf16, ones_bd, gain):
    t = t_bf16.astype(F32)
    sq = t * t
    hi = sq.astype(BF16)
    lo = (sq - hi.astype(F32)).astype(BF16)
    ssq = (jnp.dot(hi, ones_bd, preferred_element_type=F32)
           + jnp.dot(lo, ones_bd, preferred_element_type=F32))
    return ((t * lax.rsqrt(ssq * (1.0 / HEAD_DIM) + EPS)) * gain).astype(BF16)


def _attn_body(q_ref, kc_ref, kp_ref, vc_ref, vp_ref, qg_ref, kg_ref, ones_ref, o_ref, lse_ref):
    n = pl.program_id(2)
    ones_bd = ones_ref[...]
    head_masks = _head_lane_masks()

    qn = _head_norm(q_ref[...], ones_bd, qg_ref[...])
    kn = _head_norm(jnp.concatenate([kp_ref[...], kc_ref[...]], axis=0), ones_bd, kg_ref[...])
    vcat = jnp.concatenate([vp_ref[...], vc_ref[...]], axis=0)

    qi = lax.broadcasted_iota(jnp.int32, (ATT_BLK, 2 * ATT_BLK), 0)
    kj = lax.broadcasted_iota(jnp.int32, (ATT_BLK, 2 * ATT_BLK), 1)
    band = (kj >= qi) & (kj <= qi + ATT_BLK)
    first_kmin = jnp.where(n > 0, 0, ATT_BLK)

    for jb in range(2):
        rows = slice(jb * ATT_BLK, (jb + 1) * ATT_BLK)
        keys = slice(jb * ATT_BLK, (jb + 2) * ATT_BLK)
        q_blk = qn[rows]
        zero = jnp.zeros_like(q_blk)
        q4 = jnp.concatenate([jnp.where(hm, q_blk, zero) for hm in head_masks], axis=0)
        s4 = lax.dot_general(q4, kn[keys], (((1,), (1,)), ((), ())),
                             preferred_element_type=F32)
        mask = (band & (kj >= first_kmin)) if jb == 0 else band
        e_list, m_list, den_list = [], [], []
        for h in range(4):
            s = jnp.where(mask, s4[h * ATT_BLK:(h + 1) * ATT_BLK], NEG_BIG)
            m = jnp.max(s, axis=-1, keepdims=True)
            e = jnp.exp(s - m)
            m_list.append(m)
            den_list.append(jnp.sum(e, axis=-1, keepdims=True))
            e_list.append(e.astype(BF16))
        o4 = jnp.dot(jnp.concatenate(e_list, axis=0), vcat[keys],
                     preferred_element_type=F32)
        o = jnp.zeros((ATT_BLK, GROUP_W), F32)
        lse = jnp.zeros((ATT_BLK, GROUP_W), F32)
        for h, hm in enumerate(head_masks):
            inv = 1.0 / den_list[h]
            o = jnp.where(hm, o4[h * ATT_BLK:(h + 1) * ATT_BLK] * inv, o)
            lse = jnp.where(hm, m_list[h] + jnp.log(den_list[h]), lse)
        o_ref[rows, :] = o.astype(BF16)
        lse_ref[rows, :] = lse


def _attention(p, pattern, dil, qg, kg, ones_bd, batch, seq):
    length = seq // dil
    pv = p.reshape(batch, length, dil * D_IN_PROJ)
    nq = length // ATT_QROWS

    def col(base):
        return lambda b, r, n: (b, n, r * PROJ_GROUPS + base + pattern)

    def col_prev(base):
        return lambda b, r, n: (b, jnp.maximum(2 * n - 1, 0), r * PROJ_GROUPS + base + pattern)

    const = lambda b, r, n: (0, 0)
    out_map = lambda b, r, n: (b, n, r)
    o, lse = pl.pallas_call(
        _attn_body,
        out_shape=(jax.ShapeDtypeStruct((batch, length, dil * GROUP_W), BF16),
                   jax.ShapeDtypeStruct((batch, length, dil * GROUP_W), F32)),
        grid=(batch, dil, nq),
        in_specs=[
            pl.BlockSpec((None, ATT_QROWS, GROUP_W), col(Q_COL)),
            pl.BlockSpec((None, ATT_QROWS, GROUP_W), col(K_COL)),
            pl.BlockSpec((None, ATT_BLK, GROUP_W), col_prev(K_COL)),
            pl.BlockSpec((None, ATT_QROWS, GROUP_W), col(V_COL)),
            pl.BlockSpec((None, ATT_BLK, GROUP_W), col_prev(V_COL)),
            pl.BlockSpec((1, GROUP_W), const),
            pl.BlockSpec((1, GROUP_W), const),
            pl.BlockSpec((GROUP_W, GROUP_W), const),
        ],
        out_specs=(pl.BlockSpec((None, ATT_QROWS, GROUP_W), out_map),
                   pl.BlockSpec((None, ATT_QROWS, GROUP_W), out_map)),
        compiler_params=pltpu.CompilerParams(
            dimension_semantics=("arbitrary", "arbitrary", "arbitrary")),
        name=f"attn_d{dil}",
    )(pv, pv, pv, pv, pv, qg, kg, ones_bd)
    return o.reshape(batch * seq, GROUP_W), lse.reshape(batch * seq, GROUP_W)


MIX_TM = 512
MIX_TN = 512
HALO = 16


def _mix_out_body(ab_ref, halo_ref, o0_ref, o1_ref, o2_ref, l0_ref, l1_ref, l2_ref,
                  sw_ref, sb_ref, cw_ref, wo_ref, x_ref, out_ref, y_ref, *, tiles_per_seq):
    i = pl.program_id(0)

    lane = lax.broadcasted_iota(jnp.int32, (CHUNK, CHUNK), 1)
    row = lax.broadcasted_iota(jnp.int32, (CHUNK, 2 * CHUNK), 0)
    colm = lax.broadcasted_iota(jnp.int32, (CHUNK, 2 * CHUNK), 1) % CHUNK
    causal = colm <= row
    for pair in range(A_WIDTH // CHUNK):
        w_pair = jnp.where(causal, sw_ref[pair], jnp.zeros((), BF16))
        ucols = slice(pair * CHUNK, (pair + 1) * CHUNK)
        vcols = slice(A_WIDTH + pair * CHUNK, A_WIDTH + (pair + 1) * CHUNK)
        bias = sb_ref[:, ucols]
        for c in range(MIX_TM // CHUNK):
            rows = slice(c * CHUNK, (c + 1) * CHUNK)
            v = ab_ref[rows, vcols]
            zero = jnp.zeros_like(v)
            rhs = jnp.concatenate([jnp.where(lane < HEAD_DIM, v, zero),
                                   jnp.where(lane >= HEAD_DIM, v, zero)], axis=0)
            mixed = jnp.dot(w_pair, rhs, preferred_element_type=F32) + bias
            y_ref[rows, ucols] = (ab_ref[rows, ucols].astype(F32) * mixed).astype(BF16)

    b0 = 2 * A_WIDTH
    gate = ab_ref[:, b0:b0 + B_WIDTH].astype(F32)
    z = (ab_ref[:, b0 + B_WIDTH:b0 + 2 * B_WIDTH].astype(F32)
         * ab_ref[:, b0 + 2 * B_WIDTH:b0 + 3 * B_WIDTH].astype(F32))
    zh = (halo_ref[HALO - 8:, b0 + B_WIDTH:b0 + 2 * B_WIDTH].astype(F32)
          * halo_ref[HALO - 8:, b0 + 2 * B_WIDTH:b0 + 3 * B_WIDTH].astype(F32))
    zh = jnp.where(i % tiles_per_seq == 0, jnp.zeros_like(zh), zh)
    zext = jnp.concatenate([zh, z], axis=0)
    conv = (cw_ref[0:1, :] * zext[6:6 + MIX_TM]
            + cw_ref[1:2, :] * zext[7:7 + MIX_TM]
            + cw_ref[2:3, :] * z)
    y_ref[:, A_WIDTH:A_WIDTH + B_WIDTH] = (gate * conv).astype(BF16)

    l0, l1, l2 = l0_ref[...], l1_ref[...], l2_ref[...]
    m = jnp.maximum(jnp.maximum(l0, l1), l2)
    e0, e1, e2 = jnp.exp(l0 - m), jnp.exp(l1 - m), jnp.exp(l2 - m)
    inv = 1.0 / (e0 + e1 + e2)
    c0 = A_WIDTH + B_WIDTH
    for g, (o_ref, e) in enumerate(((o0_ref, e0), (o1_ref, e1), (o2_ref, e2))):
        y_ref[:, c0 + g * GROUP_W:c0 + (g + 1) * GROUP_W] = (
            o_ref[...].astype(F32) * (e * inv)).astype(BF16)

    for c in range(D_MODEL // MIX_TN):
        cols = slice(c * MIX_TN, (c + 1) * MIX_TN)
        out_ref[:, cols] = x_ref[:, cols] + jnp.dot(y_ref[...], wo_ref[:, cols],
                                                    preferred_element_type=F32)


def _mix_out(p, o_list, l_list, sw, sb, cw, wo, x, seq):
    t = x.shape[0]
    tiles_per_seq = seq // MIX_TM
    halo_blocks = MIX_TM // HALO
    row = lambda i: (i, 0)
    const2 = lambda i: (0, 0)
    est = (2 * MIX_TM * AB_WIDTH * 2 + 6 * MIX_TM * GROUP_W * 2 + 6 * MIX_TM * GROUP_W * 4
           + 2 * D_MODEL * D_MODEL * 2 + 4 * MIX_TM * D_MODEL * 4 + MIX_TM * D_MODEL * 2
           + 16 * MIX_TM * B_WIDTH * 4)
    return pl.pallas_call(
        functools.partial(_mix_out_body, tiles_per_seq=tiles_per_seq),
        out_shape=jax.ShapeDtypeStruct((t, D_MODEL), F32),
        grid=(t // MIX_TM,),
        in_specs=[
            pl.BlockSpec((MIX_TM, AB_WIDTH), row),
            pl.BlockSpec((HALO, AB_WIDTH), lambda i: (jnp.maximum(i * halo_blocks - 1, 0), 0)),
            pl.BlockSpec((MIX_TM, GROUP_W), row),
            pl.BlockSpec((MIX_TM, GROUP_W), row),
            pl.BlockSpec((MIX_TM, GROUP_W), row),
            pl.BlockSpec((MIX_TM, GROUP_W), row),
            pl.BlockSpec((MIX_TM, GROUP_W), row),
            pl.BlockSpec((MIX_TM, GROUP_W), row),
            pl.BlockSpec((A_WIDTH // CHUNK, CHUNK, 2 * CHUNK), lambda i: (0, 0, 0)),
            pl.BlockSpec((CHUNK, A_WIDTH), const2),
            pl.BlockSpec((3, B_WIDTH), const2),
            pl.BlockSpec((D_MODEL, D_MODEL), const2, pipeline_mode=pl.Buffered(1)),
            pl.BlockSpec((MIX_TM, D_MODEL), row),
        ],
        out_specs=pl.BlockSpec((MIX_TM, D_MODEL), row),
        scratch_shapes=[pltpu.VMEM((MIX_TM, D_MODEL), BF16)],
        compiler_params=pltpu.CompilerParams(
            dimension_semantics=("arbitrary",), vmem_limit_bytes=_vmem_limit(est)),
        name="mix_out",
    )(p, p, *o_list, *l_list, sw, sb, cw, wo, x)


MLP_TM = 512
MLP_TF = 1024


def _mlp_body(x_ref, g_ref, w1_ref, w2_ref, out_ref, h_ref):
    f = pl.program_id(1)

    @pl.when(f == 0)
    def _():
        h_ref[...] = _rms_norm_bf16(x_ref[...], g_ref[...])
        out_ref[...] = x_ref[...]

    u = jnp.dot(h_ref[...], w1_ref[...], preferred_element_type=F32)
    u = jnp.square(jnp.maximum(u, 0.0)).astype(BF16)
    out_ref[...] += jnp.dot(u, w2_ref[...], preferred_element_type=F32)


def _mlp(x, g, w1, w2):
    t = x.shape[0]
    est = (4 * MLP_TM * D_MODEL * 4 + 4 * D_MODEL * MLP_TF * 2 + MLP_TM * D_MODEL * 2
           + 3 * MLP_TM * MLP_TF * 4 + 2 * MLP_TM * D_MODEL * 4)
    return pl.pallas_call(
        _mlp_body,
        out_shape=jax.ShapeDtypeStruct((t, D_MODEL), F32),
        grid=(t // MLP_TM, D_FF // MLP_TF),
        in_specs=[
            pl.BlockSpec((MLP_TM, D_MODEL), lambda i, f: (i, 0)),
            pl.BlockSpec((1, D_MODEL), lambda i, f: (0, 0)),
            pl.BlockSpec((D_MODEL, MLP_TF), lambda i, f: (0, f)),
            pl.BlockSpec((MLP_TF, D_MODEL), lambda i, f: (f, 0)),
        ],
        out_specs=pl.BlockSpec((MLP_TM, D_MODEL), lambda i, f: (i, 0)),
        scratch_shapes=[pltpu.VMEM((MLP_TM, D_MODEL), BF16)],
        compiler_params=pltpu.CompilerParams(
            dimension_semantics=("arbitrary", "arbitrary"), vmem_limit_bytes=_vmem_limit(est)),
        name="mlp",
    )(x, g, w1, w2)


def kernel(x, attn_norm, w_in, sgu_w, sgu_b, conv_w, q_norm, k_norm, w_out, mlp_norm, w_mlp_in,
           w_mlp_out):
    batch, seq, d_model = x.shape
    depth = w_in.shape[0]
    xt = x.reshape(batch * seq, d_model)

    head = lax.broadcasted_iota(jnp.int32, (GROUP_W, GROUP_W), 0) // HEAD_DIM
    ones_bd = (head == head.T).astype(BF16)

    for l in range(depth):
        w_in_l = w_in[l].astype(BF16)
        w_out_l = w_out[l].astype(BF16)
        w1_l = w_mlp_in[l].astype(BF16)
        w2_l = w_mlp_out[l].astype(BF16)
        sw = sgu_w[l].reshape(A_WIDTH // CHUNK, 2, CHUNK, CHUNK).transpose(0, 2, 1, 3)
        sw = sw.reshape(A_WIDTH // CHUNK, CHUNK, 2 * CHUNK).astype(BF16)
        sb = jnp.repeat(sgu_b[l].T, HEAD_DIM, axis=1)
        qg = jnp.tile(q_norm[l], 4)[None, :] * (HEAD_DIM ** -0.5)
        kg = jnp.tile(k_norm[l], 4)[None, :]

        p = _in_proj(xt, attn_norm[l][None, :], w_in_l)
        o_list, l_list = [], []
        for pattern, dil in enumerate(DILATIONS):
            o, lse = _attention(p, pattern, dil, qg, kg, ones_bd, batch, seq)
            o_list.append(o)
            l_list.append(lse)
        xt = _mix_out(p, o_list, l_list, sw, sb, conv_w[l], w_out_l, xt, seq)
        xt = _mlp(xt, mlp_norm[l][None, :], w1_l, w2_l)
    return xt.reshape(batch, seq, d_model)
```

```python
import functools

import jax
import jax.numpy as jnp
from jax import lax
from jax.experimental import pallas as pl
from jax.experimental.pallas import tpu as pltpu

F32 = jnp.float32
BF16 = jnp.bfloat16

D_MODEL = 2048
HEAD_DIM = 64
A_WIDTH = 512
B_WIDTH = 768
CHUNK = 128
DILATIONS = (1, 4, 16)
ATT_BLK = 128
GROUP_W = 256
QKV_W = 3 * GROUP_W
PERM_BLK = 256
D_IN_PROJ = 5632
AB_WIDTH = 2 * A_WIDTH + 3 * B_WIDTH
Q_COL, K_COL, V_COL = 3328, 4096, 4864
D_FF = 4 * D_MODEL
EPS = 1e-6
NEG_BIG = -1e30

V7X_VMEM_BYTES = 64 * 1024 * 1024


def _vmem_limit(estimate_bytes):
    return int(min(V7X_VMEM_BYTES - 4 * 1024 * 1024, estimate_bytes))


def _rms_norm_bf16(x, g):
    ms = jnp.mean(x * x, axis=-1, keepdims=True)
    return ((x * lax.rsqrt(ms + EPS)) * g).astype(BF16)


def _deinterleave_matrix(dil):
    out = lax.broadcasted_iota(jnp.int32, (PERM_BLK, PERM_BLK), 0)
    src = lax.broadcasted_iota(jnp.int32, (PERM_BLK, PERM_BLK), 1)
    per = PERM_BLK // dil
    return (src == (out % per) * dil + out // per).astype(BF16)


IN_TM = 512
IN_AB_CHUNKS = ((0, 512), (512, 512), (1024, 512), (1536, 512), (2048, 512), (2560, 768))


def _in_proj_body(x_ref, g_ref, w_ref, p4_ref, p16_ref, ab_ref, q0_ref, q4_ref, q16_ref, h_ref):
    h_ref[...] = _rms_norm_bf16(x_ref[...], g_ref[...])
    for start, width in IN_AB_CHUNKS:
        cols = slice(start, start + width)
        ab_ref[:, cols] = jnp.dot(h_ref[...], w_ref[:, cols],
                                  preferred_element_type=F32).astype(BF16)
    for g, (dil, out_ref) in enumerate(zip(DILATIONS, (q0_ref, q4_ref, q16_ref))):
        cols = slice(AB_WIDTH + g * QKV_W, AB_WIDTH + (g + 1) * QKV_W)
        c = jnp.dot(h_ref[...], w_ref[:, cols], preferred_element_type=F32).astype(BF16)
        if dil == 1:
            out_ref[...] = c
            continue
        perm = (p4_ref if dil == 4 else p16_ref)[...]
        for blk in range(IN_TM // PERM_BLK):
            pc = jnp.dot(perm, c[blk * PERM_BLK:(blk + 1) * PERM_BLK],
                         preferred_element_type=F32).astype(BF16)
            out_ref[blk] = pc.reshape(dil, PERM_BLK // dil, QKV_W)


def _in_proj(x, g, w, layer, perms):
    t = x.shape[0]
    nblk = t // PERM_BLK
    tb = IN_TM // PERM_BLK
    est = (2 * IN_TM * D_MODEL * 4 + D_MODEL * D_IN_PROJ * 2 + 2 * IN_TM * D_IN_PROJ * 2
           + IN_TM * D_MODEL * 2 + 8 * IN_TM * QKV_W * 4)
    const2 = lambda i: (0, 0)
    return pl.pallas_call(
        _in_proj_body,
        out_shape=(
            jax.ShapeDtypeStruct((t, AB_WIDTH), BF16),
            jax.ShapeDtypeStruct((t, QKV_W), BF16),
            jax.ShapeDtypeStruct((nblk, 4, PERM_BLK // 4, QKV_W), BF16),
            jax.ShapeDtypeStruct((nblk, 16, PERM_BLK // 16, QKV_W), BF16),
        ),
        grid=(t // IN_TM,),
        in_specs=[
            pl.BlockSpec((IN_TM, D_MODEL), lambda i: (i, 0)),
            pl.BlockSpec((1, D_MODEL), const2),
            pl.BlockSpec((None, D_MODEL, D_IN_PROJ), lambda i: (layer, 0, 0),
                         pipeline_mode=pl.Buffered(1)),
            pl.BlockSpec((PERM_BLK, PERM_BLK), const2),
            pl.BlockSpec((PERM_BLK, PERM_BLK), const2),
        ],
        out_specs=(
            pl.BlockSpec((IN_TM, AB_WIDTH), lambda i: (i, 0)),
            pl.BlockSpec((IN_TM, QKV_W), lambda i: (i, 0)),
            pl.BlockSpec((tb, 4, PERM_BLK // 4, QKV_W), lambda i: (i, 0, 0, 0)),
            pl.BlockSpec((tb, 16, PERM_BLK // 16, QKV_W), lambda i: (i, 0, 0, 0)),
        ),
        scratch_shapes=[pltpu.VMEM((IN_TM, D_MODEL), BF16)],
        compiler_params=pltpu.CompilerParams(
            dimension_semantics=("arbitrary",), vmem_limit_bytes=_vmem_limit(est)),
        name="in_proj",
    )(x, g, w, perms[4], perms[16])


ATT_QROWS = 2 * ATT_BLK


def _head_lane_masks():
    lane = lax.broadcasted_iota(jnp.int32, (1, GROUP_W), 1)
    return [(lane >= h * HEAD_DIM) & (lane < (h + 1) * HEAD_DIM) for h in range(4)]


def _head_norm(t_bf16, ones_bd, gain):
    t = t_bf16.astype(F32)
    sq = t * t
    hi = sq.astype(BF16)
    lo = (sq - hi.astype(F32)).astype(BF16)
    ssq = (jnp.dot(hi, ones_bd, preferred_element_type=F32)
           + jnp.dot(lo, ones_bd, preferred_element_type=F32))
    return ((t * lax.rsqrt(ssq * (1.0 / HEAD_DIM) + EPS)) * gain).astype(BF16)


def _attn_body(q_ref, kc_ref, kp_ref, vc_ref, vp_ref, qg_ref, kg_ref, ones_ref, o_ref, lse_ref):
    n = pl.program_id(2)
    ones_bd = ones_ref[...]
    head_masks = _head_lane_masks()
    tile = lambda ref, rows: ref[...].reshape(rows, GROUP_W)

    qn = _head_norm(tile(q_ref, ATT_QROWS), ones_bd, qg_ref[...])
    kn = _head_norm(jnp.concatenate([tile(kp_ref, ATT_BLK), tile(kc_ref, ATT_QROWS)], axis=0),
                    ones_bd, kg_ref[...])
    vcat = jnp.concatenate([tile(vp_ref, ATT_BLK), tile(vc_ref, ATT_QROWS)], axis=0)

    qi = lax.broadcasted_iota(jnp.int32, (ATT_BLK, 2 * ATT_BLK), 0)
    kj = lax.broadcasted_iota(jnp.int32, (ATT_BLK, 2 * ATT_BLK), 1)
    band = (kj >= qi) & (kj <= qi + ATT_BLK)
    first_kmin = jnp.where(n > 0, 0, ATT_BLK)

    o_rows, lse_rows = [], []
    for jb in range(2):
        rows = slice(jb * ATT_BLK, (jb + 1) * ATT_BLK)
        keys = slice(jb * ATT_BLK, (jb + 2) * ATT_BLK)
        q_blk = qn[rows]
        zero = jnp.zeros_like(q_blk)
        q4 = jnp.concatenate([jnp.where(hm, q_blk, zero) for hm in head_masks], axis=0)
        s4 = lax.dot_general(q4, kn[keys], (((1,), (1,)), ((), ())),
                             preferred_element_type=F32)
        mask = (band & (kj >= first_kmin)) if jb == 0 else band
        e_list, m_list, den_list = [], [], []
        for h in range(4):
            s = jnp.where(mask, s4[h * ATT_BLK:(h + 1) * ATT_BLK], NEG_BIG)
            m = jnp.max(s, axis=-1, keepdims=True)
            e = jnp.exp(s - m)
            m_list.append(m)
            den_list.append(jnp.sum(e, axis=-1, keepdims=True))
            e_list.append(e.astype(BF16))
        o4 = jnp.dot(jnp.concatenate(e_list, axis=0), vcat[keys],
                     preferred_element_type=F32)
        o = jnp.zeros((ATT_BLK, GROUP_W), F32)
        lse = jnp.zeros((ATT_BLK, GROUP_W), F32)
        for h, hm in enumerate(head_masks):
            inv = 1.0 / den_list[h]
            o = jnp.where(hm, o4[h * ATT_BLK:(h + 1) * ATT_BLK] * inv, o)
            lse = jnp.where(hm, m_list[h] + jnp.log(den_list[h]), lse)
        o_rows.append(o.astype(BF16))
        lse_rows.append(lse)
    o_ref[...] = jnp.concatenate(o_rows, axis=0).reshape(o_ref.shape)
    lse_ref[...] = jnp.concatenate(lse_rows, axis=0).reshape(lse_ref.shape)


def _attention(qkv, dil, qg, kg, ones_bd, batch, seq):
    nblk = qkv.shape[0]
    per = PERM_BLK // dil
    nq = seq // (dil * ATT_QROWS)
    if dil == 1:
        half = qkv.reshape(2 * nblk, 1, ATT_BLK, QKV_W)
        half_blocks = 1
    else:
        half = qkv
        half_blocks = dil // 2
    half_rows = half.shape[2]

    def cur(col):
        return pl.BlockSpec((dil, None, per, GROUP_W), lambda b, r, n: (b * nq + n, r, 0, col))

    def prev(col):
        return pl.BlockSpec((half_blocks, None, half_rows, GROUP_W),
                            lambda b, r, n: (jnp.maximum(2 * (b * nq + n) - 1, 0), r, 0, col))

    const = lambda b, r, n: (0, 0)
    out_spec = pl.BlockSpec((dil, None, per, GROUP_W), lambda b, r, n: (b * nq + n, r, 0, 0))
    return pl.pallas_call(
        _attn_body,
        out_shape=(jax.ShapeDtypeStruct((nblk, dil, per, GROUP_W), BF16),
                   jax.ShapeDtypeStruct((nblk, dil, per, GROUP_W), F32)),
        grid=(batch, dil, nq),
        in_specs=[cur(0), cur(1), prev(1), cur(2), prev(2),
                  pl.BlockSpec((1, GROUP_W), const),
                  pl.BlockSpec((1, GROUP_W), const),
                  pl.BlockSpec((GROUP_W, GROUP_W), const)],
        out_specs=(out_spec, out_spec),
        compiler_params=pltpu.CompilerParams(
            dimension_semantics=("arbitrary", "arbitrary", "arbitrary")),
        name=f"attn_d{dil}",
    )(qkv, qkv, half, qkv, half, qg, kg, ones_bd)


MIX_TM = 512
MIX_TN = 512
HALO = 16


def _split3_bf16(v):
    hi = v.astype(BF16)
    r1 = v - hi.astype(F32)
    mid = r1.astype(BF16)
    lo = (r1 - mid.astype(F32)).astype(BF16)
    return hi, mid, lo


def _token_order(o_ref, l_ref, unperm_ref):
    rows = o_ref.shape[0] * PERM_BLK
    o = o_ref[...].reshape(rows, GROUP_W)
    lse = l_ref[...].reshape(rows, GROUP_W)
    if unperm_ref is None:
        return o.astype(F32), lse
    stacked = jnp.concatenate([o, *_split3_bf16(lse)], axis=1)
    unperm = unperm_ref[...]
    nat = jnp.concatenate(
        [jnp.dot(unperm, stacked[b * PERM_BLK:(b + 1) * PERM_BLK], preferred_element_type=F32)
         for b in range(rows // PERM_BLK)], axis=0)
    lse_nat = (nat[:, GROUP_W:2 * GROUP_W] + nat[:, 2 * GROUP_W:3 * GROUP_W]) + nat[:, 3 * GROUP_W:]
    return nat[:, :GROUP_W], lse_nat


def _mix_out_body(ab_ref, halo_ref, o0_ref, o1_ref, o2_ref, l0_ref, l1_ref, l2_ref, u4_ref, u16_ref,
                  sw_ref, sb_ref, cw_ref, wo_ref, x_ref, out_ref, y_ref, *, tiles_per_seq):
    i = pl.program_id(0)

    lane = lax.broadcasted_iota(jnp.int32, (CHUNK, CHUNK), 1)
    row = lax.broadcasted_iota(jnp.int32, (CHUNK, 2 * CHUNK), 0)
    colm = lax.broadcasted_iota(jnp.int32, (CHUNK, 2 * CHUNK), 1) % CHUNK
    causal = colm <= row
    for pair in range(A_WIDTH // CHUNK):
        w_pair = jnp.where(causal, sw_ref[pair], jnp.zeros((), BF16))
        ucols = slice(pair * CHUNK, (pair + 1) * CHUNK)
        vcols = slice(A_WIDTH + pair * CHUNK, A_WIDTH + (pair + 1) * CHUNK)
        bias = sb_ref[:, ucols]
        for c in range(MIX_TM // CHUNK):
            rows = slice(c * CHUNK, (c + 1) * CHUNK)
            v = ab_ref[rows, vcols]
            zero = jnp.zeros_like(v)
            rhs = jnp.concatenate([jnp.where(lane < HEAD_DIM, v, zero),
                                   jnp.where(lane >= HEAD_DIM, v, zero)], axis=0)
            mixed = jnp.dot(w_pair, rhs, preferred_element_type=F32) + bias
            y_ref[rows, ucols] = (ab_ref[rows, ucols].astype(F32) * mixed).astype(BF16)

    b0 = 2 * A_WIDTH
    gate = ab_ref[:, b0:b0 + B_WIDTH].astype(F32)
    z = (ab_ref[:, b0 + B_WIDTH:b0 + 2 * B_WIDTH].astype(F32)
         * ab_ref[:, b0 + 2 * B_WIDTH:b0 + 3 * B_WIDTH].astype(F32))
    zh = (halo_ref[HALO - 8:, b0 + B_WIDTH:b0 + 2 * B_WIDTH].astype(F32)
          * halo_ref[HALO - 8:, b0 + 2 * B_WIDTH:b0 + 3 * B_WIDTH].astype(F32))
    zh = jnp.where(i % tiles_per_seq == 0, jnp.zeros_like(zh), zh)
    zext = jnp.concatenate([zh, z], axis=0)
    conv = (cw_ref[0:1, :] * zext[6:6 + MIX_TM]
            + cw_ref[1:2, :] * zext[7:7 + MIX_TM]
            + cw_ref[2:3, :] * z)
    y_ref[:, A_WIDTH:A_WIDTH + B_WIDTH] = (gate * conv).astype(BF16)

    o0, l0 = _token_order(o0_ref, l0_ref, None)
    o1, l1 = _token_order(o1_ref, l1_ref, u4_ref)
    o2, l2 = _token_order(o2_ref, l2_ref, u16_ref)
    m = jnp.maximum(jnp.maximum(l0, l1), l2)
    e0, e1, e2 = jnp.exp(l0 - m), jnp.exp(l1 - m), jnp.exp(l2 - m)
    inv = 1.0 / (e0 + e1 + e2)
    c0 = A_WIDTH + B_WIDTH
    for g, (o, e) in enumerate(((o0, e0), (o1, e1), (o2, e2))):
        y_ref[:, c0 + g * GROUP_W:c0 + (g + 1) * GROUP_W] = (o * (e * inv)).astype(BF16)

    for c in range(D_MODEL // MIX_TN):
        cols = slice(c * MIX_TN, (c + 1) * MIX_TN)
        out_ref[:, cols] = x_ref[:, cols] + jnp.dot(y_ref[...], wo_ref[:, cols],
                                                    preferred_element_type=F32)


def _mix_out(ab, o_list, l_list, unperms, sw, sb, cw, wo, layer, x, seq):
    t = x.shape[0]
    tiles_per_seq = seq // MIX_TM
    halo_blocks = MIX_TM // HALO
    tb = MIX_TM // PERM_BLK
    row = lambda i: (i, 0)
    const2 = lambda i: (0, 0)

    def tile_spec(dil):
        return pl.BlockSpec((tb, dil, PERM_BLK // dil, GROUP_W), lambda i: (i, 0, 0, 0))

    est = (2 * MIX_TM * AB_WIDTH * 2 + 6 * MIX_TM * GROUP_W * 2 + 6 * MIX_TM * GROUP_W * 4
           + D_MODEL * D_MODEL * 2 + 4 * MIX_TM * D_MODEL * 4 + MIX_TM * D_MODEL * 2
           + 24 * MIX_TM * B_WIDTH * 4)
    return pl.pallas_call(
        functools.partial(_mix_out_body, tiles_per_seq=tiles_per_seq),
        out_shape=jax.ShapeDtypeStruct((t, D_MODEL), F32),
        grid=(t // MIX_TM,),
        in_specs=[
            pl.BlockSpec((MIX_TM, AB_WIDTH), row),
            pl.BlockSpec((HALO, AB_WIDTH), lambda i: (jnp.maximum(i * halo_blocks - 1, 0), 0)),
            tile_spec(1), tile_spec(4), tile_spec(16),
            tile_spec(1), tile_spec(4), tile_spec(16),
            pl.BlockSpec((PERM_BLK, PERM_BLK), const2),
            pl.BlockSpec((PERM_BLK, PERM_BLK), const2),
            pl.BlockSpec((A_WIDTH // CHUNK, CHUNK, 2 * CHUNK), lambda i: (0, 0, 0)),
            pl.BlockSpec((CHUNK, A_WIDTH), const2),
            pl.BlockSpec((3, B_WIDTH), const2),
            pl.BlockSpec((None, D_MODEL, D_MODEL), lambda i: (layer, 0, 0),
                         pipeline_mode=pl.Buffered(1)),
            pl.BlockSpec((MIX_TM, D_MODEL), row),
        ],
        out_specs=pl.BlockSpec((MIX_TM, D_MODEL), row),
        scratch_shapes=[pltpu.VMEM((MIX_TM, D_MODEL), BF16)],
        compiler_params=pltpu.CompilerParams(
            dimension_semantics=("arbitrary",), vmem_limit_bytes=_vmem_limit(est)),
        name="mix_out",
    )(ab, ab, *o_list, *l_list, unperms[4], unperms[16], sw, sb, cw, wo, x)


MLP_TM = 512
MLP_TF = 1024


def _mlp_body(x_ref, g_ref, w1_ref, w2_ref, out_ref, h_ref):
    f = pl.program_id(1)

    @pl.when(f == 0)
    def _():
        h_ref[...] = _rms_norm_bf16(x_ref[...], g_ref[...])
        out_ref[...] = x_ref[...]

    u = jnp.dot(h_ref[...], w1_ref[...], preferred_element_type=F32)
    u = jnp.square(jnp.maximum(u, 0.0)).astype(BF16)
    out_ref[...] += jnp.dot(u, w2_ref[...], preferred_element_type=F32)


def _mlp(x, g, w1, w2, layer):
    t = x.shape[0]
    est = (4 * MLP_TM * D_MODEL * 4 + 4 * D_MODEL * MLP_TF * 2 + MLP_TM * D_MODEL * 2
           + 3 * MLP_TM * MLP_TF * 4 + 2 * MLP_TM * D_MODEL * 4)
    return pl.pallas_call(
        _mlp_body,
        out_shape=jax.ShapeDtypeStruct((t, D_MODEL), F32),
        grid=(t // MLP_TM, D_FF // MLP_TF),
        in_specs=[
            pl.BlockSpec((MLP_TM, D_MODEL), lambda i, f: (i, 0)),
            pl.BlockSpec((1, D_MODEL), lambda i, f: (0, 0)),
            pl.BlockSpec((None, D_MODEL, MLP_TF), lambda i, f: (layer, 0, f)),
            pl.BlockSpec((None, MLP_TF, D_MODEL), lambda i, f: (layer, f, 0)),
        ],
        out_specs=pl.BlockSpec((MLP_TM, D_MODEL), lambda i, f: (i, 0)),
        scratch_shapes=[pltpu.VMEM((MLP_TM, D_MODEL), BF16)],
        compiler_params=pltpu.CompilerParams(
            dimension_semantics=("arbitrary", "arbitrary"), vmem_limit_bytes=_vmem_limit(est)),
        name="mlp",
    )(x, g, w1, w2)


def kernel(x, attn_norm, w_in, sgu_w, sgu_b, conv_w, q_norm, k_norm, w_out, mlp_norm, w_mlp_in,
           w_mlp_out):
    batch, seq, d_model = x.shape
    depth = w_in.shape[0]
    xt = x.reshape(batch * seq, d_model)

    head = lax.broadcasted_iota(jnp.int32, (GROUP_W, GROUP_W), 0) // HEAD_DIM
    ones_bd = (head == head.T).astype(BF16)
    perms = {dil: _deinterleave_matrix(dil) for dil in (4, 16)}
    unperms = {dil: perms[dil].T for dil in (4, 16)}

    qkv_cols = [w_in[:, :, base + g * GROUP_W:base + (g + 1) * GROUP_W]
                for g in range(len(DILATIONS)) for base in (Q_COL, K_COL, V_COL)]
    w_in_b = jnp.concatenate([w_in[:, :, :AB_WIDTH]] + qkv_cols, axis=-1).astype(BF16)
    w_out_b = w_out.astype(BF16)
    w1_b = w_mlp_in.astype(BF16)
    w2_b = w_mlp_out.astype(BF16)

    for l in range(depth):
        sw = sgu_w[l].reshape(A_WIDTH // CHUNK, 2, CHUNK, CHUNK).transpose(0, 2, 1, 3)
        sw = sw.reshape(A_WIDTH // CHUNK, CHUNK, 2 * CHUNK).astype(BF16)
        sb = jnp.repeat(sgu_b[l].T, HEAD_DIM, axis=1)
        qg = jnp.tile(q_norm[l], 4)[None, :] * (HEAD_DIM ** -0.5)
        kg = jnp.tile(k_norm[l], 4)[None, :]

        ab, qkv0, qkv4, qkv16 = _in_proj(xt, attn_norm[l][None, :], w_in_b, l, perms)
        qkv0 = qkv0.reshape(batch * seq // PERM_BLK, 1, PERM_BLK, QKV_W)
        o_list, l_list = [], []
        for dil, qkv in zip(DILATIONS, (qkv0, qkv4, qkv16)):
            o, lse = _attention(qkv, dil, qg, kg, ones_bd, batch, seq)
            o_list.append(o)
            l_list.append(lse)
        xt = _mix_out(ab, o_list, l_list, unperms, sw, sb, conv_w[l], w_out_b, l, xt, seq)
        xt = _mlp(xt, mlp_norm[l][None, :], w1_b, w2_b, l)
    return xt.reshape(batch, seq, d_model)
```

```python
import functools

import jax
import jax.numpy as jnp
from jax import lax
from jax.experimental import pallas as pl
from jax.experimental.pallas import tpu as pltpu

F32 = jnp.float32
BF16 = jnp.bfloat16

D_MODEL = 2048
HEAD_DIM = 64
A_WIDTH = 512
B_WIDTH = 768
CHUNK = 128
DILATIONS = (1, 4, 16)
ATT_BLK = 128
GROUP_W = 256
QKV_W = 3 * GROUP_W
PERM_BLK = 256
D_IN_PROJ = 5632
AB_WIDTH = 2 * A_WIDTH + 3 * B_WIDTH
Q_COL, K_COL, V_COL = 3328, 4096, 4864
D_FF = 4 * D_MODEL
EPS = 1e-6
NEG_BIG = -1e30
LOG2_E = 1.4426950408889634

V7X_VMEM_BYTES = 64 * 1024 * 1024


def _vmem_limit(estimate_bytes):
    return int(min(V7X_VMEM_BYTES - 4 * 1024 * 1024, estimate_bytes))


def _rms_norm_bf16(x, g):
    ms = jnp.mean(x * x, axis=-1, keepdims=True)
    return ((x * lax.rsqrt(ms + EPS)) * g).astype(BF16)


def _deinterleave_matrix(dil):
    out = lax.broadcasted_iota(jnp.int32, (PERM_BLK, PERM_BLK), 0)
    src = lax.broadcasted_iota(jnp.int32, (PERM_BLK, PERM_BLK), 1)
    per = PERM_BLK // dil
    return (src == (out % per) * dil + out // per).astype(BF16)


IN_TM = 512
IN_AB_CHUNKS = ((0, 512), (512, 512), (1024, 512), (1536, 512), (2048, 512), (2560, 768))


def _qk_head_norm(c, ones_bd, qg, kg):
    rows = c.shape[0]
    q, k = c[:, :GROUP_W], c[:, GROUP_W:2 * GROUP_W]
    sq = jnp.concatenate([q * q, k * k], axis=0).astype(BF16)
    ssq = jnp.dot(sq, ones_bd, preferred_element_type=F32)
    rinv = lax.rsqrt(ssq * (1.0 / HEAD_DIM) + EPS)
    qn = (q * rinv[:rows]) * qg
    kn = (k * rinv[rows:]) * kg
    return jnp.concatenate([qn.astype(BF16), kn.astype(BF16), c[:, 2 * GROUP_W:].astype(BF16)],
                           axis=1)


def _in_proj_body(x_ref, g_ref, w_ref, p4_ref, p16_ref, qg_ref, kg_ref, ones_ref,
                  ab_ref, q0_ref, q4_ref, q16_ref, h_ref):
    h_ref[...] = _rms_norm_bf16(x_ref[...], g_ref[...])
    for start, width in IN_AB_CHUNKS:
        cols = slice(start, start + width)
        ab_ref[:, cols] = jnp.dot(h_ref[...], w_ref[:, cols],
                                  preferred_element_type=F32).astype(BF16)
    for g, (dil, out_ref) in enumerate(zip(DILATIONS, (q0_ref, q4_ref, q16_ref))):
        cols = slice(AB_WIDTH + g * QKV_W, AB_WIDTH + (g + 1) * QKV_W)
        c = _qk_head_norm(jnp.dot(h_ref[...], w_ref[:, cols], preferred_element_type=F32),
                          ones_ref[...], qg_ref[...], kg_ref[...])
        if dil == 1:
            out_ref[...] = c
            continue
        perm = (p4_ref if dil == 4 else p16_ref)[...]
        for blk in range(IN_TM // PERM_BLK):
            pc = jnp.dot(perm, c[blk * PERM_BLK:(blk + 1) * PERM_BLK],
                         preferred_element_type=F32).astype(BF16)
            out_ref[blk] = pc.reshape(dil, PERM_BLK // dil, QKV_W)


def _in_proj(x, g, w, layer, perms, qg, kg, ones_bd):
    t = x.shape[0]
    nblk = t // PERM_BLK
    tb = IN_TM // PERM_BLK
    est = (2 * IN_TM * D_MODEL * 4 + D_MODEL * D_IN_PROJ * 2 + 2 * IN_TM * D_IN_PROJ * 2
           + IN_TM * D_MODEL * 2 + 8 * IN_TM * QKV_W * 4)
    const2 = lambda i: (0, 0)
    return pl.pallas_call(
        _in_proj_body,
        out_shape=(
            jax.ShapeDtypeStruct((t, AB_WIDTH), BF16),
            jax.ShapeDtypeStruct((t, QKV_W), BF16),
            jax.ShapeDtypeStruct((nblk, 4, PERM_BLK // 4, QKV_W), BF16),
            jax.ShapeDtypeStruct((nblk, 16, PERM_BLK // 16, QKV_W), BF16),
        ),
        grid=(t // IN_TM,),
        in_specs=[
            pl.BlockSpec((IN_TM, D_MODEL), lambda i: (i, 0)),
            pl.BlockSpec((1, D_MODEL), const2),
            pl.BlockSpec((None, D_MODEL, D_IN_PROJ), lambda i: (layer, 0, 0),
                         pipeline_mode=pl.Buffered(1)),
            pl.BlockSpec((PERM_BLK, PERM_BLK), const2),
            pl.BlockSpec((PERM_BLK, PERM_BLK), const2),
            pl.BlockSpec((1, GROUP_W), const2),
            pl.BlockSpec((1, GROUP_W), const2),
            pl.BlockSpec((GROUP_W, GROUP_W), const2),
        ],
        out_specs=(
            pl.BlockSpec((IN_TM, AB_WIDTH), lambda i: (i, 0)),
            pl.BlockSpec((IN_TM, QKV_W), lambda i: (i, 0)),
            pl.BlockSpec((tb, 4, PERM_BLK // 4, QKV_W), lambda i: (i, 0, 0, 0)),
            pl.BlockSpec((tb, 16, PERM_BLK // 16, QKV_W), lambda i: (i, 0, 0, 0)),
        ),
        scratch_shapes=[pltpu.VMEM((IN_TM, D_MODEL), BF16)],
        compiler_params=pltpu.CompilerParams(
            dimension_semantics=("arbitrary",), vmem_limit_bytes=_vmem_limit(est)),
        name="in_proj",
    )(x, g, w, perms[4], perms[16], qg, kg, ones_bd)


ATT_QROWS = 2 * ATT_BLK


def _head_lane_masks():
    lane = lax.broadcasted_iota(jnp.int32, (1, GROUP_W), 1)
    return [(lane >= h * HEAD_DIM) & (lane < (h + 1) * HEAD_DIM) for h in range(4)]


def _attend_block(q_blk, k_blk, v_blk, bias, head_masks):
    zero = jnp.zeros_like(q_blk)
    q4 = jnp.concatenate([jnp.where(hm, q_blk, zero) for hm in head_masks], axis=0)
    s4 = lax.dot_general(q4, k_blk, (((1,), (1,)), ((), ())), preferred_element_type=F32)
    e_list, m_list, den_list = [], [], []
    for h in range(4):
        s = s4[h * ATT_BLK:(h + 1) * ATT_BLK] + bias
        m = jnp.max(s, axis=-1, keepdims=True)
        e = jnp.exp2(s - m)
        m_list.append(m)
        den_list.append(jnp.sum(e, axis=-1, keepdims=True))
        e_list.append(e.astype(BF16))
    o4 = jnp.dot(jnp.concatenate(e_list, axis=0), v_blk, preferred_element_type=F32)
    o = jnp.zeros((ATT_BLK, GROUP_W), F32)
    lse2 = jnp.zeros((ATT_BLK, GROUP_W), F32)
    for h, hm in enumerate(head_masks):
        inv = 1.0 / den_list[h]
        o = jnp.where(hm, o4[h * ATT_BLK:(h + 1) * ATT_BLK] * inv, o)
        lse2 = jnp.where(hm, m_list[h] + jnp.log2(den_list[h]), lse2)
    return o.astype(BF16), lse2


def _attn_body(*refs):
    ins, outs = refs[:5 * len(DILATIONS)], refs[5 * len(DILATIONS):]
    s = pl.program_id(1)
    head_masks = _head_lane_masks()
    tile = lambda ref, rows: ref[...].reshape(rows, GROUP_W)
    qi = lax.broadcasted_iota(jnp.int32, (ATT_BLK, 2 * ATT_BLK), 0)
    kj = lax.broadcasted_iota(jnp.int32, (ATT_BLK, 2 * ATT_BLK), 1)
    band = (kj >= qi) & (kj <= qi + ATT_BLK)
    band_bias = jnp.where(band, 0.0, NEG_BIG)

    for g, dil in enumerate(DILATIONS):
        q_ref, kc_ref, kp_ref, vc_ref, vp_ref = ins[5 * g:5 * g + 5]
        o_ref, lse_ref = outs[2 * g:2 * g + 2]
        qn = tile(q_ref, ATT_QROWS)
        kcat = jnp.concatenate([tile(kp_ref, ATT_BLK), tile(kc_ref, ATT_QROWS)], axis=0)
        vcat = jnp.concatenate([tile(vp_ref, ATT_BLK), tile(vc_ref, ATT_QROWS)], axis=0)
        first_kmin = jnp.where(s // dil > 0, 0, ATT_BLK)
        first_bias = jnp.where(band & (kj >= first_kmin), 0.0, NEG_BIG)
        o_rows, lse_rows = [], []
        for jb in range(2):
            o, lse = _attend_block(qn[jb * ATT_BLK:(jb + 1) * ATT_BLK],
                                   kcat[jb * ATT_BLK:(jb + 2) * ATT_BLK],
                                   vcat[jb * ATT_BLK:(jb + 2) * ATT_BLK],
                                   first_bias if jb == 0 else band_bias, head_masks)
            o_rows.append(o)
            lse_rows.append(lse)
        o_ref[...] = jnp.concatenate(o_rows, axis=0).reshape(o_ref.shape)
        lse_ref[...] = jnp.concatenate(lse_rows, axis=0).reshape(lse_ref.shape)


def _attention(qkvs, batch, seq):
    steps = seq // ATT_QROWS
    operands, in_specs, out_specs, out_shapes = [], [], [], []
    for dil, qkv in zip(DILATIONS, qkvs):
        nblk = qkv.shape[0]
        per = PERM_BLK // dil
        units = steps // dil
        if dil == 1:
            half = qkv.reshape(2 * nblk, 1, ATT_BLK, QKV_W)
            half_blocks = 1
        else:
            half = qkv
            half_blocks = dil // 2
        half_rows = half.shape[2]

        def unit(b, s, dil=dil, units=units):
            return b * units + s // dil

        def cur(col, dil=dil, per=per, unit=unit):
            return pl.BlockSpec((dil, None, per, GROUP_W),
                                lambda b, s: (unit(b, s), s % dil, 0, col))

        def prev(col, dil=dil, half_blocks=half_blocks, half_rows=half_rows, unit=unit):
            return pl.BlockSpec((half_blocks, None, half_rows, GROUP_W),
                                lambda b, s: (jnp.maximum(2 * unit(b, s) - 1, 0), s % dil, 0, col))

        operands += [qkv, qkv, half, qkv, half]
        in_specs += [cur(0), cur(1), prev(1), cur(2), prev(2)]
        out_specs += [cur(0), cur(0)]
        out_shapes += [jax.ShapeDtypeStruct((nblk, dil, per, GROUP_W), BF16),
                       jax.ShapeDtypeStruct((nblk, dil, per, GROUP_W), F32)]
    outs = pl.pallas_call(
        _attn_body,
        out_shape=tuple(out_shapes),
        grid=(batch, steps),
        in_specs=in_specs,
        out_specs=tuple(out_specs),
        compiler_params=pltpu.CompilerParams(dimension_semantics=("arbitrary", "arbitrary")),
        name="attn",
    )(*operands)
    return list(outs[0::2]), list(outs[1::2])


MIX_TM = 512
MIX_TN = 512
HALO = 16


def _split3_bf16(v):
    hi = v.astype(BF16)
    r1 = v - hi.astype(F32)
    mid = r1.astype(BF16)
    lo = (r1 - mid.astype(F32)).astype(BF16)
    return hi, mid, lo


def _token_order(o_ref, l_ref, unperm_ref):
    rows = o_ref.shape[0] * PERM_BLK
    o = o_ref[...].reshape(rows, GROUP_W)
    lse = l_ref[...].reshape(rows, GROUP_W)
    if unperm_ref is None:
        return o.astype(F32), lse
    stacked = jnp.concatenate([o, *_split3_bf16(lse)], axis=1)
    unperm = unperm_ref[...]
    nat = jnp.concatenate(
        [jnp.dot(unperm, stacked[b * PERM_BLK:(b + 1) * PERM_BLK], preferred_element_type=F32)
         for b in range(rows // PERM_BLK)], axis=0)
    lse_nat = (nat[:, GROUP_W:2 * GROUP_W] + nat[:, 2 * GROUP_W:3 * GROUP_W]) + nat[:, 3 * GROUP_W:]
    return nat[:, :GROUP_W], lse_nat


def _mix_out_body(ab_ref, halo_ref, o0_ref, o1_ref, o2_ref, l0_ref, l1_ref, l2_ref, u4_ref, u16_ref,
                  sw_ref, sb_ref, cw_ref, wo_ref, x_ref, out_ref, y_ref, *, tiles_per_seq):
    i = pl.program_id(0)

    lane = lax.broadcasted_iota(jnp.int32, (CHUNK, CHUNK), 1)
    row = lax.broadcasted_iota(jnp.int32, (CHUNK, 2 * CHUNK), 0)
    colm = lax.broadcasted_iota(jnp.int32, (CHUNK, 2 * CHUNK), 1) % CHUNK
    causal = colm <= row
    for pair in range(A_WIDTH // CHUNK):
        w_pair = jnp.where(causal, sw_ref[pair], jnp.zeros((), BF16))
        ucols = slice(pair * CHUNK, (pair + 1) * CHUNK)
        vcols = slice(A_WIDTH + pair * CHUNK, A_WIDTH + (pair + 1) * CHUNK)
        bias = sb_ref[:, ucols]
        for c in range(MIX_TM // CHUNK):
            rows = slice(c * CHUNK, (c + 1) * CHUNK)
            v = ab_ref[rows, vcols]
            zero = jnp.zeros_like(v)
            rhs = jnp.concatenate([jnp.where(lane < HEAD_DIM, v, zero),
                                   jnp.where(lane >= HEAD_DIM, v, zero)], axis=0)
            mixed = jnp.dot(w_pair, rhs, preferred_element_type=F32) + bias
            y_ref[rows, ucols] = (ab_ref[rows, ucols].astype(F32) * mixed).astype(BF16)

    b0 = 2 * A_WIDTH
    gate = ab_ref[:, b0:b0 + B_WIDTH].astype(F32)
    z = (ab_ref[:, b0 + B_WIDTH:b0 + 2 * B_WIDTH].astype(F32)
         * ab_ref[:, b0 + 2 * B_WIDTH:b0 + 3 * B_WIDTH].astype(F32))
    zh = (halo_ref[HALO - 8:, b0 + B_WIDTH:b0 + 2 * B_WIDTH].astype(F32)
          * halo_ref[HALO - 8:, b0 + 2 * B_WIDTH:b0 + 3 * B_WIDTH].astype(F32))
    zh = jnp.where(i % tiles_per_seq == 0, jnp.zeros_like(zh), zh)
    zext = jnp.concatenate([zh, z], axis=0)
    conv = (cw_ref[0:1, :] * zext[6:6 + MIX_TM]
            + cw_ref[1:2, :] * zext[7:7 + MIX_TM]
            + cw_ref[2:3, :] * z)
    y_ref[:, A_WIDTH:A_WIDTH + B_WIDTH] = (gate * conv).astype(BF16)

    o0, l0 = _token_order(o0_ref, l0_ref, None)
    o1, l1 = _token_order(o1_ref, l1_ref, u4_ref)
    o2, l2 = _token_order(o2_ref, l2_ref, u16_ref)
    m = jnp.maximum(jnp.maximum(l0, l1), l2)
    e0, e1, e2 = jnp.exp2(l0 - m), jnp.exp2(l1 - m), jnp.exp2(l2 - m)
    inv = 1.0 / (e0 + e1 + e2)
    c0 = A_WIDTH + B_WIDTH
    for g, (o, e) in enumerate(((o0, e0), (o1, e1), (o2, e2))):
        y_ref[:, c0 + g * GROUP_W:c0 + (g + 1) * GROUP_W] = (o * (e * inv)).astype(BF16)

    for c in range(D_MODEL // MIX_TN):
        cols = slice(c * MIX_TN, (c + 1) * MIX_TN)
        out_ref[:, cols] = x_ref[:, cols] + jnp.dot(y_ref[...], wo_ref[:, cols],
                                                    preferred_element_type=F32)


def _mix_out(ab, o_list, l_list, unperms, sw, sb, cw, wo, layer, x, seq):
    t = x.shape[0]
    tiles_per_seq = seq // MIX_TM
    halo_blocks = MIX_TM // HALO
    tb = MIX_TM // PERM_BLK
    row = lambda i: (i, 0)
    const2 = lambda i: (0, 0)

    def tile_spec(dil):
        return pl.BlockSpec((tb, dil, PERM_BLK // dil, GROUP_W), lambda i: (i, 0, 0, 0))

    est = (2 * MIX_TM * AB_WIDTH * 2 + 6 * MIX_TM * GROUP_W * 2 + 6 * MIX_TM * GROUP_W * 4
           + D_MODEL * D_MODEL * 2 + 4 * MIX_TM * D_MODEL * 4 + MIX_TM * D_MODEL * 2
           + 24 * MIX_TM * B_WIDTH * 4)
    return pl.pallas_call(
        functools.partial(_mix_out_body, tiles_per_seq=tiles_per_seq),
        out_shape=jax.ShapeDtypeStruct((t, D_MODEL), F32),
        grid=(t // MIX_TM,),
        in_specs=[
            pl.BlockSpec((MIX_TM, AB_WIDTH), row),
            pl.BlockSpec((HALO, AB_WIDTH), lambda i: (jnp.maximum(i * halo_blocks - 1, 0), 0)),
            tile_spec(1), tile_spec(4), tile_spec(16),
            tile_spec(1), tile_spec(4), tile_spec(16),
            pl.BlockSpec((PERM_BLK, PERM_BLK), const2),
            pl.BlockSpec((PERM_BLK, PERM_BLK), const2),
            pl.BlockSpec((A_WIDTH // CHUNK, CHUNK, 2 * CHUNK), lambda i: (0, 0, 0)),
            pl.BlockSpec((CHUNK, A_WIDTH), const2),
            pl.BlockSpec((3, B_WIDTH), const2),
            pl.BlockSpec((None, D_MODEL, D_MODEL), lambda i: (layer, 0, 0),
                         pipeline_mode=pl.Buffered(1)),
            pl.BlockSpec((MIX_TM, D_MODEL), row),
        ],
        out_specs=pl.BlockSpec((MIX_TM, D_MODEL), row),
        scratch_shapes=[pltpu.VMEM((MIX_TM, D_MODEL), BF16)],
        compiler_params=pltpu.CompilerParams(
            dimension_semantics=("arbitrary",), vmem_limit_bytes=_vmem_limit(est)),
        name="mix_out",
    )(ab, ab, *o_list, *l_list, unperms[4], unperms[16], sw, sb, cw, wo, x)


MLP_TM = 512
MLP_TF = 1024


def _mlp_body(x_ref, g_ref, w1_ref, w2_ref, out_ref, h_ref):
    f = pl.program_id(1)

    @pl.when(f == 0)
    def _():
        h_ref[...] = _rms_norm_bf16(x_ref[...], g_ref[...])
        out_ref[...] = x_ref[...]

    u = jnp.dot(h_ref[...], w1_ref[...], preferred_element_type=F32)
    u = jnp.square(jnp.maximum(u, 0.0)).astype(BF16)
    out_ref[...] += jnp.dot(u, w2_ref[...], preferred_element_type=F32)


def _mlp(x, g, w1, w2, layer):
    t = x.shape[0]
    est = (4 * MLP_TM * D_MODEL * 4 + 4 * D_MODEL * MLP_TF * 2 + MLP_TM * D_MODEL * 2
           + 3 * MLP_TM * MLP_TF * 4 + 2 * MLP_TM * D_MODEL * 4)
    return pl.pallas_call(
        _mlp_body,
        out_shape=jax.ShapeDtypeStruct((t, D_MODEL), F32),
        grid=(t // MLP_TM, D_FF // MLP_TF),
        in_specs=[
            pl.BlockSpec((MLP_TM, D_MODEL), lambda i, f: (i, 0)),
            pl.BlockSpec((1, D_MODEL), lambda i, f: (0, 0)),
            pl.BlockSpec((None, D_MODEL, MLP_TF), lambda i, f: (layer, 0, f)),
            pl.BlockSpec((None, MLP_TF, D_MODEL), lambda i, f: (layer, f, 0)),
        ],
        out_specs=pl.BlockSpec((MLP_TM, D_MODEL), lambda i, f: (i, 0)),
        scratch_shapes=[pltpu.VMEM((MLP_TM, D_MODEL), BF16)],
        compiler_params=pltpu.CompilerParams(
            dimension_semantics=("arbitrary", "arbitrary"), vmem_limit_bytes=_vmem_limit(est)),
        name="mlp",
    )(x, g, w1, w2)


def kernel(x, attn_norm, w_in, sgu_w, sgu_b, conv_w, q_norm, k_norm, w_out, mlp_norm, w_mlp_in,
           w_mlp_out):
    batch, seq, d_model = x.shape
    depth = w_in.shape[0]
    xt = x.reshape(batch * seq, d_model)

    head = lax.broadcasted_iota(jnp.int32, (GROUP_W, GROUP_W), 0) // HEAD_DIM
    ones_bd = (head == head.T).astype(BF16)
    perms = {dil: _deinterleave_matrix(dil) for dil in (4, 16)}
    unperms = {dil: perms[dil].T for dil in (4, 16)}

    qkv_cols = [w_in[:, :, base + g * GROUP_W:base + (g + 1) * GROUP_W]
                for g in range(len(DILATIONS)) for base in (Q_COL, K_COL, V_COL)]
    w_in_b = jnp.concatenate([w_in[:, :, :AB_WIDTH]] + qkv_cols, axis=-1).astype(BF16)
    w_out_b = w_out.astype(BF16)
    w1_b = w_mlp_in.astype(BF16)
    w2_b = w_mlp_out.astype(BF16)

    for l in range(depth):
        sw = sgu_w[l].reshape(A_WIDTH // CHUNK, 2, CHUNK, CHUNK).transpose(0, 2, 1, 3)
        sw = sw.reshape(A_WIDTH // CHUNK, CHUNK, 2 * CHUNK).astype(BF16)
        sb = jnp.repeat(sgu_b[l].T, HEAD_DIM, axis=1)
        qg = jnp.tile(q_norm[l], 4)[None, :] * (HEAD_DIM ** -0.5 * LOG2_E)
        kg = jnp.tile(k_norm[l], 4)[None, :]

        ab, qkv0, qkv4, qkv16 = _in_proj(xt, attn_norm[l][None, :], w_in_b, l, perms, qg, kg,
                                         ones_bd)
        qkv0 = qkv0.reshape(batch * seq // PERM_BLK, 1, PERM_BLK, QKV_W)
        o_list, l_list = _attention((qkv0, qkv4, qkv16), batch, seq)
        xt = _mix_out(ab, o_list, l_list, unperms, sw, sb, conv_w[l], w_out_b, l, xt, seq)
        xt = _mlp(xt, mlp_norm[l][None, :], w1_b, w2_b, l)
    return xt.reshape(batch, seq, d_model)
```

```python
import functools

import jax
import jax.numpy as jnp
from jax import lax
from jax.experimental import pallas as pl
from jax.experimental.pallas import tpu as pltpu

F32 = jnp.float32
BF16 = jnp.bfloat16

D_MODEL = 2048
HEAD_DIM = 64
A_WIDTH = 512
B_WIDTH = 768
CHUNK = 128
DILATIONS = (1, 4, 16)
ATT_BLK = 128
GROUP_W = 256
QKV_W = 3 * GROUP_W
PERM_BLK = 256
D_IN_PROJ = 5632
AB_WIDTH = 2 * A_WIDTH + 3 * B_WIDTH
Q_COL, K_COL, V_COL = 3328, 4096, 4864
D_FF = 4 * D_MODEL
EPS = 1e-6
NEG_BIG = -1e30
LOG2_E = 1.4426950408889634

V7X_VMEM_BYTES = 64 * 1024 * 1024


def _vmem_limit(estimate_bytes):
    return int(min(V7X_VMEM_BYTES - 4 * 1024 * 1024, estimate_bytes))


def _rms_norm_bf16(x, g):
    ms = jnp.mean(x * x, axis=-1, keepdims=True)
    return ((x * lax.rsqrt(ms + EPS)) * g).astype(BF16)


def _deinterleave_matrix(dil):
    out = lax.broadcasted_iota(jnp.int32, (PERM_BLK, PERM_BLK), 0)
    src = lax.broadcasted_iota(jnp.int32, (PERM_BLK, PERM_BLK), 1)
    per = PERM_BLK // dil
    return (src == (out % per) * dil + out // per).astype(BF16)


IN_TM = 512
IN_TN = 512
IN_TAIL = (AB_WIDTH // IN_TN) * IN_TN


def _qk_head_norm(c, ones_bd, qg, kg):
    rows = c.shape[0]
    q, k = c[:, :GROUP_W], c[:, GROUP_W:2 * GROUP_W]
    sq = jnp.concatenate([q * q, k * k], axis=0).astype(BF16)
    ssq = jnp.dot(sq, ones_bd, preferred_element_type=F32)
    rinv = lax.rsqrt(ssq * (1.0 / HEAD_DIM) + EPS)
    qn = (q * rinv[:rows]) * qg
    kn = (k * rinv[rows:]) * kg
    return jnp.concatenate([qn.astype(BF16), kn.astype(BF16), c[:, 2 * GROUP_W:].astype(BF16)],
                           axis=1)


def _in_proj_body(x_ref, g_ref, w_ref, p4_ref, p16_ref, qg_ref, kg_ref, ones_ref,
                  ab_ref, q0_ref, q4_ref, q16_ref, h_ref, tail_ref):
    h_ref[...] = _rms_norm_bf16(x_ref[...], g_ref[...])
    for start in range(0, D_IN_PROJ, IN_TN):
        cols = slice(start, start + IN_TN)
        c = jnp.dot(h_ref[...], w_ref[:, cols], preferred_element_type=F32)
        if start < IN_TAIL:
            ab_ref[:, cols] = c.astype(BF16)
        else:
            tail_ref[:, start - IN_TAIL:start - IN_TAIL + IN_TN] = c
    ab_ref[:, IN_TAIL:] = tail_ref[:, :AB_WIDTH - IN_TAIL].astype(BF16)
    for g, (dil, out_ref) in enumerate(zip(DILATIONS, (q0_ref, q4_ref, q16_ref))):
        lo = AB_WIDTH - IN_TAIL + g * QKV_W
        c = _qk_head_norm(tail_ref[:, lo:lo + QKV_W], ones_ref[...], qg_ref[...], kg_ref[...])
        if dil == 1:
            out_ref[...] = c
            continue
        perm = (p4_ref if dil == 4 else p16_ref)[...]
        for blk in range(IN_TM // PERM_BLK):
            pc = jnp.dot(perm, c[blk * PERM_BLK:(blk + 1) * PERM_BLK],
                         preferred_element_type=F32).astype(BF16)
            out_ref[blk] = pc.reshape(dil, PERM_BLK // dil, QKV_W)


def _in_proj(x, g, w, perms, qg, kg, ones_bd):
    t = x.shape[0]
    nblk = t // PERM_BLK
    tb = IN_TM // PERM_BLK
    est = (2 * IN_TM * D_MODEL * 4 + D_MODEL * D_IN_PROJ * 2 + 2 * IN_TM * D_IN_PROJ * 2
           + IN_TM * D_MODEL * 2 + IN_TM * (D_IN_PROJ - IN_TAIL) * 4 + 8 * IN_TM * QKV_W * 4)
    const2 = lambda i: (0, 0)
    return pl.pallas_call(
        _in_proj_body,
        out_shape=(
            jax.ShapeDtypeStruct((t, AB_WIDTH), BF16),
            jax.ShapeDtypeStruct((t, QKV_W), BF16),
            jax.ShapeDtypeStruct((nblk, 4, PERM_BLK // 4, QKV_W), BF16),
            jax.ShapeDtypeStruct((nblk, 16, PERM_BLK // 16, QKV_W), BF16),
        ),
        grid=(t // IN_TM,),
        in_specs=[
            pl.BlockSpec((IN_TM, D_MODEL), lambda i: (i, 0)),
            pl.BlockSpec((1, D_MODEL), const2),
            pl.BlockSpec((D_MODEL, D_IN_PROJ), const2, pipeline_mode=pl.Buffered(1)),
            pl.BlockSpec((PERM_BLK, PERM_BLK), const2),
            pl.BlockSpec((PERM_BLK, PERM_BLK), const2),
            pl.BlockSpec((1, GROUP_W), const2),
            pl.BlockSpec((1, GROUP_W), const2),
            pl.BlockSpec((GROUP_W, GROUP_W), const2),
        ],
        out_specs=(
            pl.BlockSpec((IN_TM, AB_WIDTH), lambda i: (i, 0)),
            pl.BlockSpec((IN_TM, QKV_W), lambda i: (i, 0)),
            pl.BlockSpec((tb, 4, PERM_BLK // 4, QKV_W), lambda i: (i, 0, 0, 0)),
            pl.BlockSpec((tb, 16, PERM_BLK // 16, QKV_W), lambda i: (i, 0, 0, 0)),
        ),
        scratch_shapes=[pltpu.VMEM((IN_TM, D_MODEL), BF16),
                        pltpu.VMEM((IN_TM, D_IN_PROJ - IN_TAIL), F32)],
        compiler_params=pltpu.CompilerParams(
            dimension_semantics=("arbitrary",), vmem_limit_bytes=_vmem_limit(est)),
        name="in_proj",
    )(x, g, w, perms[4], perms[16], qg, kg, ones_bd)


ATT_QROWS = 2 * ATT_BLK


def _head_lane_masks():
    lane = lax.broadcasted_iota(jnp.int32, (1, GROUP_W), 1)
    return [(lane >= h * HEAD_DIM) & (lane < (h + 1) * HEAD_DIM) for h in range(4)]


def _attend_block(q_blk, k_blk, v_blk, bias, head_masks):
    zero = jnp.zeros_like(q_blk)
    q4 = jnp.concatenate([jnp.where(hm, q_blk, zero) for hm in head_masks], axis=0)
    s4 = lax.dot_general(q4, k_blk, (((1,), (1,)), ((), ())), preferred_element_type=F32)
    e_list, m_list, den_list = [], [], []
    for h in range(4):
        s = s4[h * ATT_BLK:(h + 1) * ATT_BLK] + bias
        m = jnp.max(s, axis=-1, keepdims=True)
        e = jnp.exp2(s - m)
        m_list.append(m)
        den_list.append(jnp.sum(e, axis=-1, keepdims=True))
        e_list.append(e.astype(BF16))
    o4 = jnp.dot(jnp.concatenate(e_list, axis=0), v_blk, preferred_element_type=F32)
    o = jnp.zeros((ATT_BLK, GROUP_W), F32)
    lse2 = jnp.zeros((ATT_BLK, GROUP_W), F32)
    for h, hm in enumerate(head_masks):
        inv = 1.0 / den_list[h]
        o = jnp.where(hm, o4[h * ATT_BLK:(h + 1) * ATT_BLK] * inv, o)
        lse2 = jnp.where(hm, m_list[h] + jnp.log2(den_list[h]), lse2)
    return o.astype(BF16), lse2


def _attn_body(*refs):
    ins, outs = refs[:5 * len(DILATIONS)], refs[5 * len(DILATIONS):]
    s = pl.program_id(1)
    head_masks = _head_lane_masks()
    tile = lambda ref, rows: ref[...].reshape(rows, GROUP_W)
    qi = lax.broadcasted_iota(jnp.int32, (ATT_BLK, 2 * ATT_BLK), 0)
    kj = lax.broadcasted_iota(jnp.int32, (ATT_BLK, 2 * ATT_BLK), 1)
    band = (kj >= qi) & (kj <= qi + ATT_BLK)
    band_bias = jnp.where(band, 0.0, NEG_BIG)

    for g, dil in enumerate(DILATIONS):
        q_ref, kc_ref, kp_ref, vc_ref, vp_ref = ins[5 * g:5 * g + 5]
        o_ref, lse_ref = outs[2 * g:2 * g + 2]
        qn = tile(q_ref, ATT_QROWS)
        kcat = jnp.concatenate([tile(kp_ref, ATT_BLK), tile(kc_ref, ATT_QROWS)], axis=0)
        vcat = jnp.concatenate([tile(vp_ref, ATT_BLK), tile(vc_ref, ATT_QROWS)], axis=0)
        first_kmin = jnp.where(s // dil > 0, 0, ATT_BLK)
        first_bias = jnp.where(band & (kj >= first_kmin), 0.0, NEG_BIG)
        o_rows, lse_rows = [], []
        for jb in range(2):
            o, lse = _attend_block(qn[jb * ATT_BLK:(jb + 1) * ATT_BLK],
                                   kcat[jb * ATT_BLK:(jb + 2) * ATT_BLK],
                                   vcat[jb * ATT_BLK:(jb + 2) * ATT_BLK],
                                   first_bias if jb == 0 else band_bias, head_masks)
            o_rows.append(o)
            lse_rows.append(lse)
        o_ref[...] = jnp.concatenate(o_rows, axis=0).reshape(o_ref.shape)
        lse_ref[...] = jnp.concatenate(lse_rows, axis=0).reshape(lse_ref.shape)


def _attention(qkvs, batch, seq):
    steps = seq // ATT_QROWS
    operands, in_specs, out_specs, out_shapes = [], [], [], []
    for dil, qkv in zip(DILATIONS, qkvs):
        nblk = qkv.shape[0]
        per = PERM_BLK // dil
        units = steps // dil
        if dil == 1:
            half = qkv.reshape(2 * nblk, 1, ATT_BLK, QKV_W)
            half_blocks = 1
        else:
            half = qkv
            half_blocks = dil // 2
        half_rows = half.shape[2]

        def unit(b, s, dil=dil, units=units):
            return b * units + s // dil

        def cur(col, dil=dil, per=per, unit=unit):
            return pl.BlockSpec((dil, None, per, GROUP_W),
                                lambda b, s: (unit(b, s), s % dil, 0, col))

        def prev(col, dil=dil, half_blocks=half_blocks, half_rows=half_rows, unit=unit):
            return pl.BlockSpec((half_blocks, None, half_rows, GROUP_W),
                                lambda b, s: (jnp.maximum(2 * unit(b, s) - 1, 0), s % dil, 0, col))

        operands += [qkv, qkv, half, qkv, half]
        in_specs += [cur(0), cur(1), prev(1), cur(2), prev(2)]
        out_specs += [cur(0), cur(0)]
        out_shapes += [jax.ShapeDtypeStruct((nblk, dil, per, GROUP_W), BF16),
                       jax.ShapeDtypeStruct((nblk, dil, per, GROUP_W), F32)]
    outs = pl.pallas_call(
        _attn_body,
        out_shape=tuple(out_shapes),
        grid=(batch, steps),
        in_specs=in_specs,
        out_specs=tuple(out_specs),
        compiler_params=pltpu.CompilerParams(dimension_semantics=("arbitrary", "arbitrary")),
        name="attn",
    )(*operands)
    return list(outs[0::2]), list(outs[1::2])


MIX_TM = 512
MIX_TN = 512
HALO = 16


def _split3_bf16(v):
    hi = v.astype(BF16)
    r1 = v - hi.astype(F32)
    mid = r1.astype(BF16)
    lo = (r1 - mid.astype(F32)).astype(BF16)
    return hi, mid, lo


def _token_order(o_ref, l_ref, unperm_ref, blk):
    o = o_ref[blk].reshape(PERM_BLK, GROUP_W)
    lse = l_ref[blk].reshape(PERM_BLK, GROUP_W)
    if unperm_ref is None:
        return o.astype(F32), lse
    stacked = jnp.concatenate([o, *_split3_bf16(lse)], axis=1)
    nat = jnp.dot(unperm_ref[...], stacked, preferred_element_type=F32)
    lse_nat = (nat[:, GROUP_W:2 * GROUP_W] + nat[:, 2 * GROUP_W:3 * GROUP_W]) + nat[:, 3 * GROUP_W:]
    return nat[:, :GROUP_W], lse_nat


def _mix_out_body(ab_ref, halo_ref, o0_ref, o1_ref, o2_ref, l0_ref, l1_ref, l2_ref, u4_ref, u16_ref,
                  sw_ref, sb_ref, cw_ref, wo_ref, x_ref, out_ref, ya_ref, yb_ref, yc_ref, *,
                  tiles_per_seq):
    i = pl.program_id(0)
    n_chunks = D_MODEL // MIX_TN

    def project(y_ref, k0, c):
        cols = slice(c * MIX_TN, (c + 1) * MIX_TN)
        base = x_ref[:, cols] if k0 == 0 else out_ref[:, cols]
        out_ref[:, cols] = base + jnp.dot(y_ref[...], wo_ref[k0:k0 + y_ref.shape[1], cols],
                                          preferred_element_type=F32)

    lane = lax.broadcasted_iota(jnp.int32, (CHUNK, CHUNK), 1)
    row = lax.broadcasted_iota(jnp.int32, (CHUNK, 2 * CHUNK), 0)
    colm = lax.broadcasted_iota(jnp.int32, (CHUNK, 2 * CHUNK), 1) % CHUNK
    causal = colm <= row
    for pair in range(A_WIDTH // CHUNK):
        w_pair = jnp.where(causal, sw_ref[pair], jnp.zeros((), BF16))
        ucols = slice(pair * CHUNK, (pair + 1) * CHUNK)
        vcols = slice(A_WIDTH + pair * CHUNK, A_WIDTH + (pair + 1) * CHUNK)
        bias = sb_ref[:, ucols]
        for c in range(MIX_TM // CHUNK):
            rows = slice(c * CHUNK, (c + 1) * CHUNK)
            v = ab_ref[rows, vcols]
            zero = jnp.zeros_like(v)
            rhs = jnp.concatenate([jnp.where(lane < HEAD_DIM, v, zero),
                                   jnp.where(lane >= HEAD_DIM, v, zero)], axis=0)
            mixed = jnp.dot(w_pair, rhs, preferred_element_type=F32) + bias
            ya_ref[rows, ucols] = (ab_ref[rows, ucols].astype(F32) * mixed).astype(BF16)

    def conv_group(j):
        gcols = slice(j * GROUP_W, (j + 1) * GROUP_W)
        b0 = 2 * A_WIDTH + j * GROUP_W
        gate = ab_ref[:, b0:b0 + GROUP_W].astype(F32)
        z = (ab_ref[:, b0 + B_WIDTH:b0 + B_WIDTH + GROUP_W].astype(F32)
             * ab_ref[:, b0 + 2 * B_WIDTH:b0 + 2 * B_WIDTH + GROUP_W].astype(F32))
        zh = (halo_ref[HALO - 8:, b0 + B_WIDTH:b0 + B_WIDTH + GROUP_W].astype(F32)
              * halo_ref[HALO - 8:, b0 + 2 * B_WIDTH:b0 + 2 * B_WIDTH + GROUP_W].astype(F32))
        zh = jnp.where(i % tiles_per_seq == 0, jnp.zeros_like(zh), zh)
        zext = jnp.concatenate([zh, z], axis=0)
        conv = (cw_ref[0:1, gcols] * zext[6:6 + MIX_TM]
                + cw_ref[1:2, gcols] * zext[7:7 + MIX_TM]
                + cw_ref[2:3, gcols] * z)
        yb_ref[:, gcols] = (gate * conv).astype(BF16)

    def mixture_block(blk):
        rows = slice(blk * PERM_BLK, (blk + 1) * PERM_BLK)
        o0, l0 = _token_order(o0_ref, l0_ref, None, blk)
        o1, l1 = _token_order(o1_ref, l1_ref, u4_ref, blk)
        o2, l2 = _token_order(o2_ref, l2_ref, u16_ref, blk)
        m = jnp.maximum(jnp.maximum(l0, l1), l2)
        e0, e1, e2 = jnp.exp2(l0 - m), jnp.exp2(l1 - m), jnp.exp2(l2 - m)
        inv = 1.0 / (e0 + e1 + e2)
        for g, (o, e) in enumerate(((o0, e0), (o1, e1), (o2, e2))):
            yc_ref[rows, g * GROUP_W:(g + 1) * GROUP_W] = (o * (e * inv)).astype(BF16)

    vector_after_a = [functools.partial(conv_group, j) for j in range(B_WIDTH // GROUP_W)]
    vector_after_b = [functools.partial(mixture_block, b) for b in range(MIX_TM // PERM_BLK)]
    for y_ref, k0, tasks in ((ya_ref, 0, vector_after_a),
                             (yb_ref, A_WIDTH, vector_after_b),
                             (yc_ref, A_WIDTH + B_WIDTH, [])):
        for c in range(n_chunks):
            project(y_ref, k0, c)
            if c < len(tasks):
                tasks[c]()


def _mix_out(ab, o_list, l_list, unperms, sw, sb, cw, wo, x, seq):
    t = x.shape[0]
    tiles_per_seq = seq // MIX_TM
    halo_blocks = MIX_TM // HALO
    tb = MIX_TM // PERM_BLK
    row = lambda i: (i, 0)
    const2 = lambda i: (0, 0)

    def tile_spec(dil):
        return pl.BlockSpec((tb, dil, PERM_BLK // dil, GROUP_W), lambda i: (i, 0, 0, 0))

    est = (2 * MIX_TM * AB_WIDTH * 2 + 6 * MIX_TM * GROUP_W * 2 + 6 * MIX_TM * GROUP_W * 4
           + D_MODEL * D_MODEL * 2 + 4 * MIX_TM * D_MODEL * 4 + MIX_TM * D_MODEL * 2
           + 24 * MIX_TM * B_WIDTH * 4)
    return pl.pallas_call(
        functools.partial(_mix_out_body, tiles_per_seq=tiles_per_seq),
        out_shape=jax.ShapeDtypeStruct((t, D_MODEL), F32),
        grid=(t // MIX_TM,),
        in_specs=[
            pl.BlockSpec((MIX_TM, AB_WIDTH), row),
            pl.BlockSpec((HALO, AB_WIDTH), lambda i: (jnp.maximum(i * halo_blocks - 1, 0), 0)),
            tile_spec(1), tile_spec(4), tile_spec(16),
            tile_spec(1), tile_spec(4), tile_spec(16),
            pl.BlockSpec((PERM_BLK, PERM_BLK), const2),
            pl.BlockSpec((PERM_BLK, PERM_BLK), const2),
            pl.BlockSpec((A_WIDTH // CHUNK, CHUNK, 2 * CHUNK), lambda i: (0, 0, 0)),
            pl.BlockSpec((CHUNK, A_WIDTH), const2),
            pl.BlockSpec((3, B_WIDTH), const2),
            pl.BlockSpec((D_MODEL, D_MODEL), const2, pipeline_mode=pl.Buffered(1)),
            pl.BlockSpec((MIX_TM, D_MODEL), row),
        ],
        out_specs=pl.BlockSpec((MIX_TM, D_MODEL), row),
        scratch_shapes=[pltpu.VMEM((MIX_TM, A_WIDTH), BF16), pltpu.VMEM((MIX_TM, B_WIDTH), BF16),
                        pltpu.VMEM((MIX_TM, len(DILATIONS) * GROUP_W), BF16)],
        compiler_params=pltpu.CompilerParams(
            dimension_semantics=("arbitrary",), vmem_limit_bytes=_vmem_limit(est)),
        name="mix_out",
    )(ab, ab, *o_list, *l_list, unperms[4], unperms[16], sw, sb, cw, wo, x)


MLP_TM = 512
MLP_TF = 1024


MLP_NF = D_FF // MLP_TF
MLP_NORM_ROWS = MLP_TM // MLP_NF


def _regroup_in_proj_cols(w):
    parts = [w[..., :AB_WIDTH]]
    for g in range(len(DILATIONS)):
        parts += [w[..., base + g * GROUP_W:base + (g + 1) * GROUP_W]
                  for base in (Q_COL, K_COL, V_COL)]
    return jnp.concatenate(parts, axis=-1)


def _mlp_body(*refs, n_cast):
    x_ref, xn_ref, g_ref, w1_ref, w2_ref = refs[:5]
    cast_in = refs[5:5 + n_cast]
    out_ref = refs[5 + n_cast]
    cast_out = refs[6 + n_cast:6 + 2 * n_cast]
    h_ref, hn_ref = refs[6 + 2 * n_cast:]
    i, f = pl.program_id(0), pl.program_id(1)

    @pl.when((i == 0) & (f == 0))
    def _():
        h_ref[...] = _rms_norm_bf16(x_ref[...], g_ref[...])

    @pl.when((i > 0) & (f == 0))
    def _():
        h_ref[...] = hn_ref[...]

    @pl.when(f == 0)
    def _():
        out_ref[...] = x_ref[...]

    r0 = pl.multiple_of(f * MLP_NORM_ROWS, MLP_NORM_ROWS)
    hn_ref[pl.ds(r0, MLP_NORM_ROWS), :] = _rms_norm_bf16(
        xn_ref[pl.ds(r0, MLP_NORM_ROWS), :], g_ref[...])

    for k, (src, dst) in enumerate(zip(cast_in, cast_out)):
        w = src[...]
        dst[...] = (_regroup_in_proj_cols(w) if k == 0 else w).astype(BF16)

    u = jnp.dot(h_ref[...], w1_ref[...], preferred_element_type=F32)
    u = jnp.square(jnp.maximum(u, 0.0)).astype(BF16)
    out_ref[...] += jnp.dot(u, w2_ref[...], preferred_element_type=F32)


def _mlp(x, g, w1, w2, next_weights=None):
    t = x.shape[0]
    n_i = t // MLP_TM
    steps = n_i * MLP_NF
    cast_ops, cast_specs, cast_out_specs, cast_shapes = [], [], [], []
    if next_weights is not None:
        *stacked, layer = next_weights
        for w in stacked:
            rows, cols = w.shape[1] // steps, w.shape[2]
            cast_ops.append(w)
            cast_specs.append(pl.BlockSpec((None, rows, cols),
                                           lambda i, f: (layer, i * MLP_NF + f, 0)))
            cast_out_specs.append(pl.BlockSpec((rows, cols), lambda i, f: (i * MLP_NF + f, 0)))
            cast_shapes.append(jax.ShapeDtypeStruct(w.shape[1:], BF16))
    cast_bytes = sum(6 * w.shape[1] // steps * w.shape[2] * 2 for w in cast_ops)
    est = (6 * MLP_TM * D_MODEL * 4 + 4 * D_MODEL * MLP_TF * 2 + 2 * MLP_TM * D_MODEL * 2
           + 3 * MLP_TM * MLP_TF * 4 + MLP_TM * D_MODEL * 4 + cast_bytes)
    tile = lambda i, f: (i, 0)
    outs = pl.pallas_call(
        functools.partial(_mlp_body, n_cast=len(cast_ops)),
        out_shape=(jax.ShapeDtypeStruct((t, D_MODEL), F32), *cast_shapes),
        grid=(n_i, MLP_NF),
        in_specs=[
            pl.BlockSpec((MLP_TM, D_MODEL), tile),
            pl.BlockSpec((MLP_TM, D_MODEL), lambda i, f: (jnp.minimum(i + 1, n_i - 1), 0)),
            pl.BlockSpec((1, D_MODEL), lambda i, f: (0, 0)),
            pl.BlockSpec((D_MODEL, MLP_TF), lambda i, f: (0, f)),
            pl.BlockSpec((MLP_TF, D_MODEL), lambda i, f: (f, 0)),
            *cast_specs,
        ],
        out_specs=(pl.BlockSpec((MLP_TM, D_MODEL), tile), *cast_out_specs),
        scratch_shapes=[pltpu.VMEM((MLP_TM, D_MODEL), BF16), pltpu.VMEM((MLP_TM, D_MODEL), BF16)],
        compiler_params=pltpu.CompilerParams(
            dimension_semantics=("arbitrary", "arbitrary"), vmem_limit_bytes=_vmem_limit(est)),
        name="mlp",
    )(x, x, g, w1, w2, *cast_ops)
    return outs[0], tuple(outs[1:])


def kernel(x, attn_norm, w_in, sgu_w, sgu_b, conv_w, q_norm, k_norm, w_out, mlp_norm, w_mlp_in,
           w_mlp_out):
    batch, seq, d_model = x.shape
    depth = w_in.shape[0]
    xt = x.reshape(batch * seq, d_model)

    head = lax.broadcasted_iota(jnp.int32, (GROUP_W, GROUP_W), 0) // HEAD_DIM
    ones_bd = (head == head.T).astype(BF16)
    perms = {dil: _deinterleave_matrix(dil) for dil in (4, 16)}
    unperms = {dil: perms[dil].T for dil in (4, 16)}

    weights = (_regroup_in_proj_cols(w_in[0]).astype(BF16), w_out[0].astype(BF16),
               w_mlp_in[0].astype(BF16), w_mlp_out[0].astype(BF16))

    for l in range(depth):
        w_in_b, w_out_b, w1_b, w2_b = weights
        sw = sgu_w[l].reshape(A_WIDTH // CHUNK, 2, CHUNK, CHUNK).transpose(0, 2, 1, 3)
        sw = sw.reshape(A_WIDTH // CHUNK, CHUNK, 2 * CHUNK).astype(BF16)
        sb = jnp.repeat(sgu_b[l].T, HEAD_DIM, axis=1)
        qg = jnp.tile(q_norm[l], 4)[None, :] * (HEAD_DIM ** -0.5 * LOG2_E)
        kg = jnp.tile(k_norm[l], 4)[None, :]

        ab, qkv0, qkv4, qkv16 = _in_proj(xt, attn_norm[l][None, :], w_in_b, perms, qg, kg, ones_bd)
        qkv0 = qkv0.reshape(batch * seq // PERM_BLK, 1, PERM_BLK, QKV_W)
        o_list, l_list = _attention((qkv0, qkv4, qkv16), batch, seq)
        xt = _mix_out(ab, o_list, l_list, unperms, sw, sb, conv_w[l], w_out_b, xt, seq)
        nxt = (w_in, w_out, w_mlp_in, w_mlp_out, l + 1) if l + 1 < depth else None
        xt, weights = _mlp(xt, mlp_norm[l][None, :], w1_b, w2_b, nxt)
    return xt.reshape(batch, seq, d_model)
```

```python
import functools

import jax
import jax.numpy as jnp
from jax import lax
from jax.experimental import pallas as pl
from jax.experimental.pallas import tpu as pltpu

F32 = jnp.float32
BF16 = jnp.bfloat16

D_MODEL = 2048
HEAD_DIM = 64
A_WIDTH = 512
B_WIDTH = 768
CHUNK = 128
DILATIONS = (1, 4, 16)
ATT_BLK = 128
GROUP_W = 256
QKV_W = 3 * GROUP_W
PERM_BLK = 256
D_IN_PROJ = 5632
AB_WIDTH = 2 * A_WIDTH + 3 * B_WIDTH
Q_COL, K_COL, V_COL = 3328, 4096, 4864
D_FF = 4 * D_MODEL
EPS = 1e-6
NEG_BIG = -1e30
LOG2_E = 1.4426950408889634

V7X_VMEM_BYTES = 64 * 1024 * 1024


def _vmem_limit(estimate_bytes):
    return int(min(V7X_VMEM_BYTES - 4 * 1024 * 1024, estimate_bytes))


def _rms_norm_bf16(x, g):
    ms = jnp.mean(x * x, axis=-1, keepdims=True)
    return ((x * lax.rsqrt(ms + EPS)) * g).astype(BF16)


def _deinterleave_matrix(dil):
    out = lax.broadcasted_iota(jnp.int32, (PERM_BLK, PERM_BLK), 0)
    src = lax.broadcasted_iota(jnp.int32, (PERM_BLK, PERM_BLK), 1)
    per = PERM_BLK // dil
    return (src == (out % per) * dil + out // per).astype(BF16)


def _regroup_in_proj_cols(w):
    parts = [w[..., :AB_WIDTH]]
    for g in range(len(DILATIONS)):
        parts += [w[..., base + g * GROUP_W:base + (g + 1) * GROUP_W]
                  for base in (Q_COL, K_COL, V_COL)]
    return jnp.concatenate(parts, axis=-1)


class _WeightCast:
    def __init__(self, stacked, layer, steps, step_index, regroup=False):
        rows, cols = stacked.shape[1] // steps, stacked.shape[2]
        self.operand = stacked
        self.regroup = regroup
        self.in_spec = pl.BlockSpec((None, rows, cols),
                                    lambda *ids: (layer, step_index(*ids), 0))
        self.out_spec = pl.BlockSpec((rows, cols), lambda *ids: (step_index(*ids), 0))
        self.out_shape = jax.ShapeDtypeStruct(stacked.shape[1:], BF16)
        self.vmem_bytes = 2 * rows * cols * (4 + 2)

    def emit(self, src_ref, dst_ref):
        w = src_ref[...]
        dst_ref[...] = (_regroup_in_proj_cols(w) if self.regroup else w).astype(BF16)


IN_TM = 512
IN_TN = 512
IN_TAIL = (AB_WIDTH // IN_TN) * IN_TN


def _qk_head_norm(c, ones_bd, qg, kg):
    rows = c.shape[0]
    q, k = c[:, :GROUP_W], c[:, GROUP_W:2 * GROUP_W]
    sq = jnp.concatenate([q * q, k * k], axis=0).astype(BF16)
    ssq = jnp.dot(sq, ones_bd, preferred_element_type=F32)
    rinv = lax.rsqrt(ssq * (1.0 / HEAD_DIM) + EPS)
    qn = (q * rinv[:rows]) * qg
    kn = (k * rinv[rows:]) * kg
    return jnp.concatenate([qn.astype(BF16), kn.astype(BF16), c[:, 2 * GROUP_W:].astype(BF16)],
                           axis=1)


def _in_proj_body(x_ref, g_ref, w_ref, p4_ref, p16_ref, qg_ref, kg_ref, ones_ref, cast_src,
                  ab_ref, q0_ref, q4_ref, q16_ref, cast_dst, h_ref, tail_ref, *, cast):
    cast.emit(cast_src, cast_dst)
    h_ref[...] = _rms_norm_bf16(x_ref[...], g_ref[...])
    for start in range(0, D_IN_PROJ, IN_TN):
        cols = slice(start, start + IN_TN)
        c = jnp.dot(h_ref[...], w_ref[:, cols], preferred_element_type=F32)
        if start < IN_TAIL:
            ab_ref[:, cols] = c.astype(BF16)
        else:
            tail_ref[:, start - IN_TAIL:start - IN_TAIL + IN_TN] = c
    ab_ref[:, IN_TAIL:] = tail_ref[:, :AB_WIDTH - IN_TAIL].astype(BF16)
    for g, (dil, out_ref) in enumerate(zip(DILATIONS, (q0_ref, q4_ref, q16_ref))):
        lo = AB_WIDTH - IN_TAIL + g * QKV_W
        c = _qk_head_norm(tail_ref[:, lo:lo + QKV_W], ones_ref[...], qg_ref[...], kg_ref[...])
        if dil == 1:
            out_ref[...] = c
            continue
        perm = (p4_ref if dil == 4 else p16_ref)[...]
        for blk in range(IN_TM // PERM_BLK):
            pc = jnp.dot(perm, c[blk * PERM_BLK:(blk + 1) * PERM_BLK],
                         preferred_element_type=F32).astype(BF16)
            out_ref[blk] = pc.reshape(dil, PERM_BLK // dil, QKV_W)


def _in_proj(x, g, w, perms, qg, kg, ones_bd, w_out_stacked, layer):
    t = x.shape[0]
    cast = _WeightCast(w_out_stacked, layer, t // IN_TM, lambda i: i)
    nblk = t // PERM_BLK
    tb = IN_TM // PERM_BLK
    est = (2 * IN_TM * D_MODEL * 4 + D_MODEL * D_IN_PROJ * 2 + 2 * IN_TM * D_IN_PROJ * 2
           + IN_TM * D_MODEL * 2 + IN_TM * (D_IN_PROJ - IN_TAIL) * 4 + 8 * IN_TM * QKV_W * 4
           + cast.vmem_bytes)
    const2 = lambda i: (0, 0)
    return pl.pallas_call(
        functools.partial(_in_proj_body, cast=cast),
        out_shape=(
            jax.ShapeDtypeStruct((t, AB_WIDTH), BF16),
            jax.ShapeDtypeStruct((t, QKV_W), BF16),
            jax.ShapeDtypeStruct((nblk, 4, PERM_BLK // 4, QKV_W), BF16),
            jax.ShapeDtypeStruct((nblk, 16, PERM_BLK // 16, QKV_W), BF16),
            cast.out_shape,
        ),
        grid=(t // IN_TM,),
        in_specs=[
            pl.BlockSpec((IN_TM, D_MODEL), lambda i: (i, 0)),
            pl.BlockSpec((1, D_MODEL), const2),
            pl.BlockSpec((D_MODEL, D_IN_PROJ), const2, pipeline_mode=pl.Buffered(1)),
            pl.BlockSpec((PERM_BLK, PERM_BLK), const2),
            pl.BlockSpec((PERM_BLK, PERM_BLK), const2),
            pl.BlockSpec((1, GROUP_W), const2),
            pl.BlockSpec((1, GROUP_W), const2),
            pl.BlockSpec((GROUP_W, GROUP_W), const2),
            cast.in_spec,
        ],
        out_specs=(
            pl.BlockSpec((IN_TM, AB_WIDTH), lambda i: (i, 0)),
            pl.BlockSpec((IN_TM, QKV_W), lambda i: (i, 0)),
            pl.BlockSpec((tb, 4, PERM_BLK // 4, QKV_W), lambda i: (i, 0, 0, 0)),
            pl.BlockSpec((tb, 16, PERM_BLK // 16, QKV_W), lambda i: (i, 0, 0, 0)),
            cast.out_spec,
        ),
        scratch_shapes=[pltpu.VMEM((IN_TM, D_MODEL), BF16),
                        pltpu.VMEM((IN_TM, D_IN_PROJ - IN_TAIL), F32)],
        compiler_params=pltpu.CompilerParams(
            dimension_semantics=("arbitrary",), vmem_limit_bytes=_vmem_limit(est)),
        name="in_proj",
    )(x, g, w, perms[4], perms[16], qg, kg, ones_bd, cast.operand)


ATT_QROWS = 2 * ATT_BLK


def _head_lane_masks():
    lane = lax.broadcasted_iota(jnp.int32, (1, GROUP_W), 1)
    return [(lane >= h * HEAD_DIM) & (lane < (h + 1) * HEAD_DIM) for h in range(4)]


def _attend_block(q_blk, k_blk, v_blk, bias, head_masks):
    zero = jnp.zeros_like(q_blk)
    q4 = jnp.concatenate([jnp.where(hm, q_blk, zero) for hm in head_masks], axis=0)
    s4 = lax.dot_general(q4, k_blk, (((1,), (1,)), ((), ())), preferred_element_type=F32)
    e_list, m_list, den_list = [], [], []
    for h in range(4):
        s = s4[h * ATT_BLK:(h + 1) * ATT_BLK] + bias
        m = jnp.max(s, axis=-1, keepdims=True)
        e = jnp.exp2(s - m)
        m_list.append(m)
        den_list.append(jnp.sum(e, axis=-1, keepdims=True))
        e_list.append(e.astype(BF16))
    o4 = jnp.dot(jnp.concatenate(e_list, axis=0), v_blk, preferred_element_type=F32)
    o = jnp.zeros((ATT_BLK, GROUP_W), F32)
    lse2 = jnp.zeros((ATT_BLK, GROUP_W), F32)
    for h, hm in enumerate(head_masks):
        inv = 1.0 / den_list[h]
        o = jnp.where(hm, o4[h * ATT_BLK:(h + 1) * ATT_BLK] * inv, o)
        lse2 = jnp.where(hm, m_list[h] + jnp.log2(den_list[h]), lse2)
    return o.astype(BF16), lse2


def _attn_body(*refs, casts):
    n_in = 5 * len(DILATIONS)
    ins, cast_src = refs[:n_in], refs[n_in:n_in + len(casts)]
    outs = refs[n_in + len(casts):]
    cast_dst = outs[2 * len(DILATIONS):]
    for cast, src, dst in zip(casts, cast_src, cast_dst):
        cast.emit(src, dst)
    s = pl.program_id(1)
    head_masks = _head_lane_masks()
    tile = lambda ref, rows: ref[...].reshape(rows, GROUP_W)
    qi = lax.broadcasted_iota(jnp.int32, (ATT_BLK, 2 * ATT_BLK), 0)
    kj = lax.broadcasted_iota(jnp.int32, (ATT_BLK, 2 * ATT_BLK), 1)
    band = (kj >= qi) & (kj <= qi + ATT_BLK)
    band_bias = jnp.where(band, 0.0, NEG_BIG)

    for g, dil in enumerate(DILATIONS):
        q_ref, kc_ref, kp_ref, vc_ref, vp_ref = ins[5 * g:5 * g + 5]
        o_ref, lse_ref = outs[2 * g:2 * g + 2]
        qn = tile(q_ref, ATT_QROWS)
        kcat = jnp.concatenate([tile(kp_ref, ATT_BLK), tile(kc_ref, ATT_QROWS)], axis=0)
        vcat = jnp.concatenate([tile(vp_ref, ATT_BLK), tile(vc_ref, ATT_QROWS)], axis=0)
        first_kmin = jnp.where(s // dil > 0, 0, ATT_BLK)
        first_bias = jnp.where(band & (kj >= first_kmin), 0.0, NEG_BIG)
        o_rows, lse_rows = [], []
        for jb in range(2):
            o, lse = _attend_block(qn[jb * ATT_BLK:(jb + 1) * ATT_BLK],
                                   kcat[jb * ATT_BLK:(jb + 2) * ATT_BLK],
                                   vcat[jb * ATT_BLK:(jb + 2) * ATT_BLK],
                                   first_bias if jb == 0 else band_bias, head_masks)
            o_rows.append(o)
            lse_rows.append(lse)
        o_ref[...] = jnp.concatenate(o_rows, axis=0).reshape(o_ref.shape)
        lse_ref[...] = jnp.concatenate(lse_rows, axis=0).reshape(lse_ref.shape)


def _attention(qkvs, batch, seq, cast_weights, layer):
    steps = seq // ATT_QROWS
    casts = [_WeightCast(w, layer, batch * steps, lambda b, s: b * steps + s)
             for w in cast_weights]
    operands, in_specs, out_specs, out_shapes = [], [], [], []
    for dil, qkv in zip(DILATIONS, qkvs):
        nblk = qkv.shape[0]
        per = PERM_BLK // dil
        units = steps // dil
        if dil == 1:
            half = qkv.reshape(2 * nblk, 1, ATT_BLK, QKV_W)
            half_blocks = 1
        else:
            half = qkv
            half_blocks = dil // 2
        half_rows = half.shape[2]

        def unit(b, s, dil=dil, units=units):
            return b * units + s // dil

        def cur(col, dil=dil, per=per, unit=unit):
            return pl.BlockSpec((dil, None, per, GROUP_W),
                                lambda b, s: (unit(b, s), s % dil, 0, col))

        def prev(col, dil=dil, half_blocks=half_blocks, half_rows=half_rows, unit=unit):
            return pl.BlockSpec((half_blocks, None, half_rows, GROUP_W),
                                lambda b, s: (jnp.maximum(2 * unit(b, s) - 1, 0), s % dil, 0, col))

        operands += [qkv, qkv, half, qkv, half]
        in_specs += [cur(0), cur(1), prev(1), cur(2), prev(2)]
        out_specs += [cur(0), cur(0)]
        out_shapes += [jax.ShapeDtypeStruct((nblk, dil, per, GROUP_W), BF16),
                       jax.ShapeDtypeStruct((nblk, dil, per, GROUP_W), F32)]
    n_att = len(out_shapes)
    outs = pl.pallas_call(
        functools.partial(_attn_body, casts=casts),
        out_shape=(*out_shapes, *[c.out_shape for c in casts]),
        grid=(batch, steps),
        in_specs=in_specs + [c.in_spec for c in casts],
        out_specs=(*out_specs, *[c.out_spec for c in casts]),
        compiler_params=pltpu.CompilerParams(dimension_semantics=("arbitrary", "arbitrary")),
        name="attn",
    )(*operands, *[c.operand for c in casts])
    return list(outs[0:n_att:2]), list(outs[1:n_att:2]), tuple(outs[n_att:])


MIX_TM = 512
MIX_TN = 512
HALO = 16


def _split3_bf16(v):
    hi = v.astype(BF16)
    r1 = v - hi.astype(F32)
    mid = r1.astype(BF16)
    lo = (r1 - mid.astype(F32)).astype(BF16)
    return hi, mid, lo


def _token_order(o_ref, l_ref, unperm_ref):
    rows = o_ref.shape[0] * PERM_BLK
    o = o_ref[...].reshape(rows, GROUP_W)
    lse = l_ref[...].reshape(rows, GROUP_W)
    if unperm_ref is None:
        return o.astype(F32), lse
    stacked = jnp.concatenate([o, *_split3_bf16(lse)], axis=1)
    unperm = unperm_ref[...]
    nat = jnp.concatenate(
        [jnp.dot(unperm, stacked[b * PERM_BLK:(b + 1) * PERM_BLK], preferred_element_type=F32)
         for b in range(rows // PERM_BLK)], axis=0)
    lse_nat = (nat[:, GROUP_W:2 * GROUP_W] + nat[:, 2 * GROUP_W:3 * GROUP_W]) + nat[:, 3 * GROUP_W:]
    return nat[:, :GROUP_W], lse_nat


def _mix_out_body(ab_ref, halo_ref, o0_ref, o1_ref, o2_ref, l0_ref, l1_ref, l2_ref, u4_ref, u16_ref,
                  sw_ref, sb_ref, cw_ref, wo_ref, x_ref, out_ref, y_ref, *, tiles_per_seq):
    i = pl.program_id(0)

    lane = lax.broadcasted_iota(jnp.int32, (CHUNK, CHUNK), 1)
    row = lax.broadcasted_iota(jnp.int32, (CHUNK, 2 * CHUNK), 0)
    colm = lax.broadcasted_iota(jnp.int32, (CHUNK, 2 * CHUNK), 1) % CHUNK
    causal = colm <= row
    for pair in range(A_WIDTH // CHUNK):
        w_pair = jnp.where(causal, sw_ref[pair], jnp.zeros((), BF16))
        ucols = slice(pair * CHUNK, (pair + 1) * CHUNK)
        vcols = slice(A_WIDTH + pair * CHUNK, A_WIDTH + (pair + 1) * CHUNK)
        bias = sb_ref[:, ucols]
        for c in range(MIX_TM // CHUNK):
            rows = slice(c * CHUNK, (c + 1) * CHUNK)
            v = ab_ref[rows, vcols]
            zero = jnp.zeros_like(v)
            rhs = jnp.concatenate([jnp.where(lane < HEAD_DIM, v, zero),
                                   jnp.where(lane >= HEAD_DIM, v, zero)], axis=0)
            mixed = jnp.dot(w_pair, rhs, preferred_element_type=F32) + bias
            y_ref[rows, ucols] = (ab_ref[rows, ucols].astype(F32) * mixed).astype(BF16)

    b0 = 2 * A_WIDTH
    gate = ab_ref[:, b0:b0 + B_WIDTH].astype(F32)
    z = (ab_ref[:, b0 + B_WIDTH:b0 + 2 * B_WIDTH].astype(F32)
         * ab_ref[:, b0 + 2 * B_WIDTH:b0 + 3 * B_WIDTH].astype(F32))
    zh = (halo_ref[HALO - 8:, b0 + B_WIDTH:b0 + 2 * B_WIDTH].astype(F32)
          * halo_ref[HALO - 8:, b0 + 2 * B_WIDTH:b0 + 3 * B_WIDTH].astype(F32))
    zh = jnp.where(i % tiles_per_seq == 0, jnp.zeros_like(zh), zh)
    zext = jnp.concatenate([zh, z], axis=0)
    conv = (cw_ref[0:1, :] * zext[6:6 + MIX_TM]
            + cw_ref[1:2, :] * zext[7:7 + MIX_TM]
            + cw_ref[2:3, :] * z)
    y_ref[:, A_WIDTH:A_WIDTH + B_WIDTH] = (gate * conv).astype(BF16)

    o0, l0 = _token_order(o0_ref, l0_ref, None)
    o1, l1 = _token_order(o1_ref, l1_ref, u4_ref)
    o2, l2 = _token_order(o2_ref, l2_ref, u16_ref)
    m = jnp.maximum(jnp.maximum(l0, l1), l2)
    e0, e1, e2 = jnp.exp2(l0 - m), jnp.exp2(l1 - m), jnp.exp2(l2 - m)
    inv = 1.0 / (e0 + e1 + e2)
    c0 = A_WIDTH + B_WIDTH
    for g, (o, e) in enumerate(((o0, e0), (o1, e1), (o2, e2))):
        y_ref[:, c0 + g * GROUP_W:c0 + (g + 1) * GROUP_W] = (o * (e * inv)).astype(BF16)

    for c in range(D_MODEL // MIX_TN):
        cols = slice(c * MIX_TN, (c + 1) * MIX_TN)
        out_ref[:, cols] = x_ref[:, cols] + jnp.dot(y_ref[...], wo_ref[:, cols],
                                                    preferred_element_type=F32)


def _mix_out(ab, o_list, l_list, unperms, sw, sb, cw, wo, x, seq):
    t = x.shape[0]
    tiles_per_seq = seq // MIX_TM
    halo_blocks = MIX_TM // HALO
    tb = MIX_TM // PERM_BLK
    row = lambda i: (i, 0)
    const2 = lambda i: (0, 0)

    def tile_spec(dil):
        return pl.BlockSpec((tb, dil, PERM_BLK // dil, GROUP_W), lambda i: (i, 0, 0, 0))

    est = (2 * MIX_TM * AB_WIDTH * 2 + 6 * MIX_TM * GROUP_W * 2 + 6 * MIX_TM * GROUP_W * 4
           + D_MODEL * D_MODEL * 2 + 4 * MIX_TM * D_MODEL * 4 + MIX_TM * D_MODEL * 2
           + 24 * MIX_TM * B_WIDTH * 4)
    return pl.pallas_call(
        functools.partial(_mix_out_body, tiles_per_seq=tiles_per_seq),
        out_shape=jax.ShapeDtypeStruct((t, D_MODEL), F32),
        grid=(t // MIX_TM,),
        in_specs=[
            pl.BlockSpec((MIX_TM, AB_WIDTH), row),
            pl.BlockSpec((HALO, AB_WIDTH), lambda i: (jnp.maximum(i * halo_blocks - 1, 0), 0)),
            tile_spec(1), tile_spec(4), tile_spec(16),
            tile_spec(1), tile_spec(4), tile_spec(16),
            pl.BlockSpec((PERM_BLK, PERM_BLK), const2),
            pl.BlockSpec((PERM_BLK, PERM_BLK), const2),
            pl.BlockSpec((A_WIDTH // CHUNK, CHUNK, 2 * CHUNK), lambda i: (0, 0, 0)),
            pl.BlockSpec((CHUNK, A_WIDTH), const2),
            pl.BlockSpec((3, B_WIDTH), const2),
            pl.BlockSpec((D_MODEL, D_MODEL), const2, pipeline_mode=pl.Buffered(1)),
            pl.BlockSpec((MIX_TM, D_MODEL), row),
        ],
        out_specs=pl.BlockSpec((MIX_TM, D_MODEL), row),
        scratch_shapes=[pltpu.VMEM((MIX_TM, D_MODEL), BF16)],
        compiler_params=pltpu.CompilerParams(
            dimension_semantics=("arbitrary",), vmem_limit_bytes=_vmem_limit(est)),
        name="mix_out",
    )(ab, ab, *o_list, *l_list, unperms[4], unperms[16], sw, sb, cw, wo, x)


MLP_TM = 512
MLP_TF = 1024
MLP_NF = D_FF // MLP_TF


def _mlp_body(*refs, casts):
    x_ref, g_ref, w1_ref, w2_ref = refs[:4]
    cast_src = refs[4:4 + len(casts)]
    out_ref = refs[4 + len(casts)]
    cast_dst = refs[5 + len(casts):5 + 2 * len(casts)]
    h_ref = refs[5 + 2 * len(casts)]

    @pl.when(pl.program_id(1) == 0)
    def _():
        h_ref[...] = _rms_norm_bf16(x_ref[...], g_ref[...])
        out_ref[...] = x_ref[...]

    for cast, src, dst in zip(casts, cast_src, cast_dst):
        cast.emit(src, dst)

    u = jnp.dot(h_ref[...], w1_ref[...], preferred_element_type=F32)
    u = jnp.square(jnp.maximum(u, 0.0)).astype(BF16)
    out_ref[...] += jnp.dot(u, w2_ref[...], preferred_element_type=F32)


def _mlp(x, g, w1, w2, next_w_in=None):
    t = x.shape[0]
    n_i = t // MLP_TM
    casts = []
    if next_w_in is not None:
        casts.append(_WeightCast(next_w_in[0], next_w_in[1], n_i * MLP_NF,
                                 lambda i, f: i * MLP_NF + f, regroup=True))
    est = (4 * MLP_TM * D_MODEL * 4 + 4 * D_MODEL * MLP_TF * 2 + MLP_TM * D_MODEL * 2
           + 3 * MLP_TM * MLP_TF * 4 + 2 * MLP_TM * D_MODEL * 4
           + sum(c.vmem_bytes for c in casts))
    tile = lambda i, f: (i, 0)
    outs = pl.pallas_call(
        functools.partial(_mlp_body, casts=casts),
        out_shape=(jax.ShapeDtypeStruct((t, D_MODEL), F32), *[c.out_shape for c in casts]),
        grid=(n_i, MLP_NF),
        in_specs=[
            pl.BlockSpec((MLP_TM, D_MODEL), tile),
            pl.BlockSpec((1, D_MODEL), lambda i, f: (0, 0)),
            pl.BlockSpec((D_MODEL, MLP_TF), lambda i, f: (0, f)),
            pl.BlockSpec((MLP_TF, D_MODEL), lambda i, f: (f, 0)),
            *[c.in_spec for c in casts],
        ],
        out_specs=(pl.BlockSpec((MLP_TM, D_MODEL), tile), *[c.out_spec for c in casts]),
        scratch_shapes=[pltpu.VMEM((MLP_TM, D_MODEL), BF16)],
        compiler_params=pltpu.CompilerParams(
            dimension_semantics=("arbitrary", "arbitrary"), vmem_limit_bytes=_vmem_limit(est)),
        name="mlp",
    )(x, g, w1, w2, *[c.operand for c in casts])
    return outs[0], (outs[1] if casts else None)


def kernel(x, attn_norm, w_in, sgu_w, sgu_b, conv_w, q_norm, k_norm, w_out, mlp_norm, w_mlp_in,
           w_mlp_out):
    batch, seq, d_model = x.shape
    depth = w_in.shape[0]
    xt = x.reshape(batch * seq, d_model)

    head = lax.broadcasted_iota(jnp.int32, (GROUP_W, GROUP_W), 0) // HEAD_DIM
    ones_bd = (head == head.T).astype(BF16)
    perms = {dil: _deinterleave_matrix(dil) for dil in (4, 16)}
    unperms = {dil: perms[dil].T for dil in (4, 16)}

    w_in_b = _regroup_in_proj_cols(w_in[0]).astype(BF16)

    for l in range(depth):
        sw = sgu_w[l].reshape(A_WIDTH // CHUNK, 2, CHUNK, CHUNK).transpose(0, 2, 1, 3)
        sw = sw.reshape(A_WIDTH // CHUNK, CHUNK, 2 * CHUNK).astype(BF16)
        sb = jnp.repeat(sgu_b[l].T, HEAD_DIM, axis=1)
        qg = jnp.tile(q_norm[l], 4)[None, :] * (HEAD_DIM ** -0.5 * LOG2_E)
        kg = jnp.tile(k_norm[l], 4)[None, :]

        ab, qkv0, qkv4, qkv16, w_out_b = _in_proj(xt, attn_norm[l][None, :], w_in_b, perms, qg, kg,
                                                  ones_bd, w_out, l)
        qkv0 = qkv0.reshape(batch * seq // PERM_BLK, 1, PERM_BLK, QKV_W)
        o_list, l_list, (w1_b, w2_b) = _attention((qkv0, qkv4, qkv16), batch, seq,
                                                  (w_mlp_in, w_mlp_out), l)
        xt = _mix_out(ab, o_list, l_list, unperms, sw, sb, conv_w[l], w_out_b, xt, seq)
        xt, w_in_b = _mlp(xt, mlp_norm[l][None, :], w1_b, w2_b,
                          (w_in, l + 1) if l + 1 < depth else None)
    return xt.reshape(batch, seq, d_model)
```

```python
import functools

import jax
import jax.numpy as jnp
from jax import lax
from jax.experimental import pallas as pl
from jax.experimental.pallas import tpu as pltpu

F32 = jnp.float32
BF16 = jnp.bfloat16

D_MODEL = 2048
HEAD_DIM = 64
A_WIDTH = 512
B_WIDTH = 768
CHUNK = 128
DILATIONS = (1, 4, 16)
ATT_BLK = 128
GROUP_W = 256
QKV_W = 3 * GROUP_W
PERM_BLK = 256
D_IN_PROJ = 5632
AB_WIDTH = 2 * A_WIDTH + 3 * B_WIDTH
Q_COL, K_COL, V_COL = 3328, 4096, 4864
D_FF = 4 * D_MODEL
EPS = 1e-6
NEG_BIG = -1e30
LOG2_E = 1.4426950408889634

V7X_VMEM_BYTES = 64 * 1024 * 1024


def _vmem_limit(estimate_bytes):
    return int(min(V7X_VMEM_BYTES - 4 * 1024 * 1024, estimate_bytes))


def _rms_norm_bf16(x, g):
    ms = jnp.mean(x * x, axis=-1, keepdims=True)
    return ((x * lax.rsqrt(ms + EPS)) * g).astype(BF16)


def _deinterleave_matrix(dil):
    out = lax.broadcasted_iota(jnp.int32, (PERM_BLK, PERM_BLK), 0)
    src = lax.broadcasted_iota(jnp.int32, (PERM_BLK, PERM_BLK), 1)
    per = PERM_BLK // dil
    return (src == (out % per) * dil + out // per).astype(BF16)


def _regroup_in_proj_cols(w):
    parts = [w[..., :AB_WIDTH]]
    for g in range(len(DILATIONS)):
        parts += [w[..., base + g * GROUP_W:base + (g + 1) * GROUP_W]
                  for base in (Q_COL, K_COL, V_COL)]
    return jnp.concatenate(parts, axis=-1)


class _WeightCast:
    def __init__(self, stacked, layer, steps, step_index, regroup=False):
        rows, cols = stacked.shape[1] // steps, stacked.shape[2]
        self.operand = stacked
        self.regroup = regroup
        self.in_spec = pl.BlockSpec((None, rows, cols),
                                    lambda *ids: (layer, step_index(*ids), 0))
        self.out_spec = pl.BlockSpec((rows, cols), lambda *ids: (step_index(*ids), 0))
        self.out_shape = jax.ShapeDtypeStruct(stacked.shape[1:], BF16)
        self.vmem_bytes = 2 * rows * cols * (4 + 2)

    def emit(self, src_ref, dst_ref):
        w = src_ref[...]
        dst_ref[...] = (_regroup_in_proj_cols(w) if self.regroup else w).astype(BF16)


IN_TM = 512
IN_TN = 512
IN_TAIL = (AB_WIDTH // IN_TN) * IN_TN


def _qk_head_norm(c, ones_bd, qg, kg):
    rows = c.shape[0]
    q, k = c[:, :GROUP_W], c[:, GROUP_W:2 * GROUP_W]
    sq = jnp.concatenate([q * q, k * k], axis=0).astype(BF16)
    ssq = jnp.dot(sq, ones_bd, preferred_element_type=F32)
    rinv = lax.rsqrt(ssq * (1.0 / HEAD_DIM) + EPS)
    qn = (q * rinv[:rows]) * qg
    kn = (k * rinv[rows:]) * kg
    return jnp.concatenate([qn.astype(BF16), kn.astype(BF16), c[:, 2 * GROUP_W:].astype(BF16)],
                           axis=1)


def _in_proj_body(x_ref, g_ref, w_ref, p4_ref, p16_ref, qg_ref, kg_ref, ones_ref, cast_src,
                  ab_ref, q0_ref, q4_ref, q16_ref, cast_dst, h_ref, tail_ref, *, cast):
    cast.emit(cast_src, cast_dst)
    h_ref[...] = _rms_norm_bf16(x_ref[...], g_ref[...])
    for start in range(0, D_IN_PROJ, IN_TN):
        cols = slice(start, start + IN_TN)
        c = jnp.dot(h_ref[...], w_ref[:, cols], preferred_element_type=F32)
        if start < IN_TAIL:
            ab_ref[:, cols] = c.astype(BF16)
        else:
            tail_ref[:, start - IN_TAIL:start - IN_TAIL + IN_TN] = c
    ab_ref[:, IN_TAIL:] = tail_ref[:, :AB_WIDTH - IN_TAIL].astype(BF16)
    for g, (dil, out_ref) in enumerate(zip(DILATIONS, (q0_ref, q4_ref, q16_ref))):
        lo = AB_WIDTH - IN_TAIL + g * QKV_W
        c = _qk_head_norm(tail_ref[:, lo:lo + QKV_W], ones_ref[...], qg_ref[...], kg_ref[...])
        if dil == 1:
            out_ref[...] = c
            continue
        perm = (p4_ref if dil == 4 else p16_ref)[...]
        for blk in range(IN_TM // PERM_BLK):
            pc = jnp.dot(perm, c[blk * PERM_BLK:(blk + 1) * PERM_BLK],
                         preferred_element_type=F32).astype(BF16)
            out_ref[blk] = pc.reshape(dil, PERM_BLK // dil, QKV_W)


def _in_proj(x, g, w, perms, qg, kg, ones_bd, w_out_stacked, layer):
    t = x.shape[0]
    cast = _WeightCast(w_out_stacked, layer, t // IN_TM, lambda i: i)
    nblk = t // PERM_BLK
    tb = IN_TM // PERM_BLK
    est = (2 * IN_TM * D_MODEL * 4 + D_MODEL * D_IN_PROJ * 2 + 2 * IN_TM * D_IN_PROJ * 2
           + IN_TM * D_MODEL * 2 + IN_TM * (D_IN_PROJ - IN_TAIL) * 4 + 8 * IN_TM * QKV_W * 4
           + cast.vmem_bytes)
    const2 = lambda i: (0, 0)
    return pl.pallas_call(
        functools.partial(_in_proj_body, cast=cast),
        out_shape=(
            jax.ShapeDtypeStruct((t, AB_WIDTH), BF16),
            jax.ShapeDtypeStruct((t, QKV_W), BF16),
            jax.ShapeDtypeStruct((nblk, 4, PERM_BLK // 4, QKV_W), BF16),
            jax.ShapeDtypeStruct((nblk, 16, PERM_BLK // 16, QKV_W), BF16),
            cast.out_shape,
        ),
        grid=(t // IN_TM,),
        in_specs=[
            pl.BlockSpec((IN_TM, D_MODEL), lambda i: (i, 0)),
            pl.BlockSpec((1, D_MODEL), const2),
            pl.BlockSpec((D_MODEL, D_IN_PROJ), const2, pipeline_mode=pl.Buffered(1)),
            pl.BlockSpec((PERM_BLK, PERM_BLK), const2),
            pl.BlockSpec((PERM_BLK, PERM_BLK), const2),
            pl.BlockSpec((1, GROUP_W), const2),
            pl.BlockSpec((1, GROUP_W), const2),
            pl.BlockSpec((GROUP_W, GROUP_W), const2),
            cast.in_spec,
        ],
        out_specs=(
            pl.BlockSpec((IN_TM, AB_WIDTH), lambda i: (i, 0)),
            pl.BlockSpec((IN_TM, QKV_W), lambda i: (i, 0)),
            pl.BlockSpec((tb, 4, PERM_BLK // 4, QKV_W), lambda i: (i, 0, 0, 0)),
            pl.BlockSpec((tb, 16, PERM_BLK // 16, QKV_W), lambda i: (i, 0, 0, 0)),
            cast.out_spec,
        ),
        scratch_shapes=[pltpu.VMEM((IN_TM, D_MODEL), BF16),
                        pltpu.VMEM((IN_TM, D_IN_PROJ - IN_TAIL), F32)],
        compiler_params=pltpu.CompilerParams(
            dimension_semantics=("arbitrary",), vmem_limit_bytes=_vmem_limit(est)),
        name="in_proj",
    )(x, g, w, perms[4], perms[16], qg, kg, ones_bd, cast.operand)


ATT_QROWS = 2 * ATT_BLK


def _head_lane_masks():
    lane = lax.broadcasted_iota(jnp.int32, (1, GROUP_W), 1)
    return [(lane >= h * HEAD_DIM) & (lane < (h + 1) * HEAD_DIM) for h in range(4)]


def _attend_block(q_blk, k_blk, v_blk, bias, head_masks):
    zero = jnp.zeros_like(q_blk)
    q4 = jnp.concatenate([jnp.where(hm, q_blk, zero) for hm in head_masks], axis=0)
    s4 = lax.dot_general(q4, k_blk, (((1,), (1,)), ((), ())), preferred_element_type=F32)
    e_list, m_list, den_list = [], [], []
    for h in range(4):
        s = s4[h * ATT_BLK:(h + 1) * ATT_BLK] + bias
        m = jnp.max(s, axis=-1, keepdims=True)
        e = jnp.exp2(s - m)
        m_list.append(m)
        den_list.append(jnp.sum(e, axis=-1, keepdims=True))
        e_list.append(e.astype(BF16))
    o4 = jnp.dot(jnp.concatenate(e_list, axis=0), v_blk, preferred_element_type=F32)
    o = jnp.zeros((ATT_BLK, GROUP_W), F32)
    lse2 = jnp.zeros((ATT_BLK, GROUP_W), F32)
    for h, hm in enumerate(head_masks):
        inv = 1.0 / den_list[h]
        o = jnp.where(hm, o4[h * ATT_BLK:(h + 1) * ATT_BLK] * inv, o)
        lse2 = jnp.where(hm, m_list[h] + jnp.log2(den_list[h]), lse2)
    return o.astype(BF16), lse2


def _attn_body(*refs, casts):
    n_in = 5 * len(DILATIONS)
    ins, cast_src = refs[:n_in], refs[n_in:n_in + len(casts)]
    outs = refs[n_in + len(casts):]
    cast_dst = outs[2 * len(DILATIONS):]
    for cast, src, dst in zip(casts, cast_src, cast_dst):
        cast.emit(src, dst)
    s = pl.program_id(1)
    head_masks = _head_lane_masks()
    tile = lambda ref, rows: ref[...].reshape(rows, GROUP_W)
    qi = lax.broadcasted_iota(jnp.int32, (ATT_BLK, 2 * ATT_BLK), 0)
    kj = lax.broadcasted_iota(jnp.int32, (ATT_BLK, 2 * ATT_BLK), 1)
    band = (kj >= qi) & (kj <= qi + ATT_BLK)
    band_bias = jnp.where(band, 0.0, NEG_BIG)

    for g, dil in enumerate(DILATIONS):
        q_ref, kc_ref, kp_ref, vc_ref, vp_ref = ins[5 * g:5 * g + 5]
        o_ref, lse_ref = outs[2 * g:2 * g + 2]
        qn = tile(q_ref, ATT_QROWS)
        kcat = jnp.concatenate([tile(kp_ref, ATT_BLK), tile(kc_ref, ATT_QROWS)], axis=0)
        vcat = jnp.concatenate([tile(vp_ref, ATT_BLK), tile(vc_ref, ATT_QROWS)], axis=0)
        first_kmin = jnp.where(s // dil > 0, 0, ATT_BLK)
        first_bias = jnp.where(band & (kj >= first_kmin), 0.0, NEG_BIG)
        o_rows, lse_rows = [], []
        for jb in range(2):
            o, lse = _attend_block(qn[jb * ATT_BLK:(jb + 1) * ATT_BLK],
                                   kcat[jb * ATT_BLK:(jb + 2) * ATT_BLK],
                                   vcat[jb * ATT_BLK:(jb + 2) * ATT_BLK],
                                   first_bias if jb == 0 else band_bias, head_masks)
            o_rows.append(o)
            lse_rows.append(lse)
        o_ref[...] = jnp.concatenate(o_rows, axis=0).reshape(o_ref.shape)
        lse_ref[...] = jnp.concatenate(lse_rows, axis=0).reshape(lse_ref.shape)


def _attention(qkvs, batch, seq, cast_weights=()):
    steps = seq // ATT_QROWS
    casts = [_WeightCast(w, layer, batch * steps, lambda b, s: b * steps + s)
             for w, layer in cast_weights]
    operands, in_specs, out_specs, out_shapes = [], [], [], []
    for dil, qkv in zip(DILATIONS, qkvs):
        nblk = qkv.shape[0]
        per = PERM_BLK // dil
        units = steps // dil
        if dil == 1:
            half = qkv.reshape(2 * nblk, 1, ATT_BLK, QKV_W)
            half_blocks = 1
        else:
            half = qkv
            half_blocks = dil // 2
        half_rows = half.shape[2]

        def unit(b, s, dil=dil, units=units):
            return b * units + s // dil

        def cur(col, dil=dil, per=per, unit=unit):
            return pl.BlockSpec((dil, None, per, GROUP_W),
                                lambda b, s: (unit(b, s), s % dil, 0, col))

        def prev(col, dil=dil, half_blocks=half_blocks, half_rows=half_rows, unit=unit):
            return pl.BlockSpec((half_blocks, None, half_rows, GROUP_W),
                                lambda b, s: (jnp.maximum(2 * unit(b, s) - 1, 0), s % dil, 0, col))

        operands += [qkv, qkv, half, qkv, half]
        in_specs += [cur(0), cur(1), prev(1), cur(2), prev(2)]
        out_specs += [cur(0), cur(0)]
        out_shapes += [jax.ShapeDtypeStruct((nblk, dil, per, GROUP_W), BF16),
                       jax.ShapeDtypeStruct((nblk, dil, per, GROUP_W), F32)]
    n_att = len(out_shapes)
    outs = pl.pallas_call(
        functools.partial(_attn_body, casts=casts),
        out_shape=(*out_shapes, *[c.out_shape for c in casts]),
        grid=(batch, steps),
        in_specs=in_specs + [c.in_spec for c in casts],
        out_specs=(*out_specs, *[c.out_spec for c in casts]),
        compiler_params=pltpu.CompilerParams(dimension_semantics=("arbitrary", "arbitrary")),
        name="attn",
    )(*operands, *[c.operand for c in casts])
    return list(outs[0:n_att:2]), list(outs[1:n_att:2]), tuple(outs[n_att:])


MIX_TM = 512
MIX_TN = 512
HALO = 16


def _split3_bf16(v):
    hi = v.astype(BF16)
    r1 = v - hi.astype(F32)
    mid = r1.astype(BF16)
    lo = (r1 - mid.astype(F32)).astype(BF16)
    return hi, mid, lo


def _token_order(o_ref, l_ref, unperm_ref):
    rows = o_ref.shape[0] * PERM_BLK
    o = o_ref[...].reshape(rows, GROUP_W)
    lse = l_ref[...].reshape(rows, GROUP_W)
    if unperm_ref is None:
        return o.astype(F32), lse
    stacked = jnp.concatenate([o, *_split3_bf16(lse)], axis=1)
    unperm = unperm_ref[...]
    nat = jnp.concatenate(
        [jnp.dot(unperm, stacked[b * PERM_BLK:(b + 1) * PERM_BLK], preferred_element_type=F32)
         for b in range(rows // PERM_BLK)], axis=0)
    lse_nat = (nat[:, GROUP_W:2 * GROUP_W] + nat[:, 2 * GROUP_W:3 * GROUP_W]) + nat[:, 3 * GROUP_W:]
    return nat[:, :GROUP_W], lse_nat


def _mix_out_body(ab_ref, halo_ref, o0_ref, o1_ref, o2_ref, l0_ref, l1_ref, l2_ref, u4_ref, u16_ref,
                  sw_ref, sb_ref, cw_ref, wo_ref, x_ref, *rest, tiles_per_seq, casts):
    cast_src, out_ref = rest[:len(casts)], rest[len(casts)]
    cast_dst, y_ref = rest[len(casts) + 1:2 * len(casts) + 1], rest[2 * len(casts) + 1]
    for cast, src, dst in zip(casts, cast_src, cast_dst):
        cast.emit(src, dst)
    i = pl.program_id(0)

    lane = lax.broadcasted_iota(jnp.int32, (CHUNK, CHUNK), 1)
    row = lax.broadcasted_iota(jnp.int32, (CHUNK, 2 * CHUNK), 0)
    colm = lax.broadcasted_iota(jnp.int32, (CHUNK, 2 * CHUNK), 1) % CHUNK
    causal = colm <= row
    for pair in range(A_WIDTH // CHUNK):
        w_pair = jnp.where(causal, sw_ref[pair], jnp.zeros((), BF16))
        ucols = slice(pair * CHUNK, (pair + 1) * CHUNK)
        vcols = slice(A_WIDTH + pair * CHUNK, A_WIDTH + (pair + 1) * CHUNK)
        bias = sb_ref[:, ucols]
        for c in range(MIX_TM // CHUNK):
            rows = slice(c * CHUNK, (c + 1) * CHUNK)
            v = ab_ref[rows, vcols]
            zero = jnp.zeros_like(v)
            rhs = jnp.concatenate([jnp.where(lane < HEAD_DIM, v, zero),
                                   jnp.where(lane >= HEAD_DIM, v, zero)], axis=0)
            mixed = jnp.dot(w_pair, rhs, preferred_element_type=F32) + bias
            y_ref[rows, ucols] = (ab_ref[rows, ucols].astype(F32) * mixed).astype(BF16)

    b0 = 2 * A_WIDTH
    gate = ab_ref[:, b0:b0 + B_WIDTH].astype(F32)
    z = (ab_ref[:, b0 + B_WIDTH:b0 + 2 * B_WIDTH].astype(F32)
         * ab_ref[:, b0 + 2 * B_WIDTH:b0 + 3 * B_WIDTH].astype(F32))
    zh = (halo_ref[HALO - 8:, b0 + B_WIDTH:b0 + 2 * B_WIDTH].astype(F32)
          * halo_ref[HALO - 8:, b0 + 2 * B_WIDTH:b0 + 3 * B_WIDTH].astype(F32))
    zh = jnp.where(i % tiles_per_seq == 0, jnp.zeros_like(zh), zh)
    zext = jnp.concatenate([zh, z], axis=0)
    conv = (cw_ref[0:1, :] * zext[6:6 + MIX_TM]
            + cw_ref[1:2, :] * zext[7:7 + MIX_TM]
            + cw_ref[2:3, :] * z)
    y_ref[:, A_WIDTH:A_WIDTH + B_WIDTH] = (gate * conv).astype(BF16)

    o0, l0 = _token_order(o0_ref, l0_ref, None)
    o1, l1 = _token_order(o1_ref, l1_ref, u4_ref)
    o2, l2 = _token_order(o2_ref, l2_ref, u16_ref)
    m = jnp.maximum(jnp.maximum(l0, l1), l2)
    e0, e1, e2 = jnp.exp2(l0 - m), jnp.exp2(l1 - m), jnp.exp2(l2 - m)
    inv = 1.0 / (e0 + e1 + e2)
    c0 = A_WIDTH + B_WIDTH
    for g, (o, e) in enumerate(((o0, e0), (o1, e1), (o2, e2))):
        y_ref[:, c0 + g * GROUP_W:c0 + (g + 1) * GROUP_W] = (o * (e * inv)).astype(BF16)

    for c in range(D_MODEL // MIX_TN):
        cols = slice(c * MIX_TN, (c + 1) * MIX_TN)
        out_ref[:, cols] = x_ref[:, cols] + jnp.dot(y_ref[...], wo_ref[:, cols],
                                                    preferred_element_type=F32)


def _mix_out(ab, o_list, l_list, unperms, sw, sb, cw, wo, x, seq, cast_weights=()):
    t = x.shape[0]
    casts = [_WeightCast(w, layer, t // MIX_TM, lambda i: i) for w, layer in cast_weights]
    tiles_per_seq = seq // MIX_TM
    halo_blocks = MIX_TM // HALO
    tb = MIX_TM // PERM_BLK
    row = lambda i: (i, 0)
    const2 = lambda i: (0, 0)

    def tile_spec(dil):
        return pl.BlockSpec((tb, dil, PERM_BLK // dil, GROUP_W), lambda i: (i, 0, 0, 0))

    est = (2 * MIX_TM * AB_WIDTH * 2 + 6 * MIX_TM * GROUP_W * 2 + 6 * MIX_TM * GROUP_W * 4
           + D_MODEL * D_MODEL * 2 + 4 * MIX_TM * D_MODEL * 4 + MIX_TM * D_MODEL * 2
           + 24 * MIX_TM * B_WIDTH * 4 + sum(c.vmem_bytes for c in casts))
    outs = pl.pallas_call(
        functools.partial(_mix_out_body, tiles_per_seq=tiles_per_seq, casts=casts),
        out_shape=(jax.ShapeDtypeStruct((t, D_MODEL), F32), *[c.out_shape for c in casts]),
        grid=(t // MIX_TM,),
        in_specs=[
            pl.BlockSpec((MIX_TM, AB_WIDTH), row),
            pl.BlockSpec((HALO, AB_WIDTH), lambda i: (jnp.maximum(i * halo_blocks - 1, 0), 0)),
            tile_spec(1), tile_spec(4), tile_spec(16),
            tile_spec(1), tile_spec(4), tile_spec(16),
            pl.BlockSpec((PERM_BLK, PERM_BLK), const2),
            pl.BlockSpec((PERM_BLK, PERM_BLK), const2),
            pl.BlockSpec((A_WIDTH // CHUNK, CHUNK, 2 * CHUNK), lambda i: (0, 0, 0)),
            pl.BlockSpec((CHUNK, A_WIDTH), const2),
            pl.BlockSpec((3, B_WIDTH), const2),
            pl.BlockSpec((D_MODEL, D_MODEL), const2, pipeline_mode=pl.Buffered(1)),
            pl.BlockSpec((MIX_TM, D_MODEL), row),
            *[c.in_spec for c in casts],
        ],
        out_specs=(pl.BlockSpec((MIX_TM, D_MODEL), row), *[c.out_spec for c in casts]),
        scratch_shapes=[pltpu.VMEM((MIX_TM, D_MODEL), BF16)],
        compiler_params=pltpu.CompilerParams(
            dimension_semantics=("arbitrary",), vmem_limit_bytes=_vmem_limit(est)),
        name="mix_out",
    )(ab, ab, *o_list, *l_list, unperms[4], unperms[16], sw, sb, cw, wo, x,
      *[c.operand for c in casts])
    return outs[0], tuple(outs[1:])


MLP_TM = 512
MLP_TF = 1024
MLP_NF = D_FF // MLP_TF


def _mlp_body(*refs, casts):
    x_ref, g_ref, w1_ref, w2_ref = refs[:4]
    cast_src = refs[4:4 + len(casts)]
    out_ref = refs[4 + len(casts)]
    cast_dst = refs[5 + len(casts):5 + 2 * len(casts)]
    h_ref = refs[5 + 2 * len(casts)]

    @pl.when(pl.program_id(1) == 0)
    def _():
        h_ref[...] = _rms_norm_bf16(x_ref[...], g_ref[...])
        out_ref[...] = x_ref[...]

    for cast, src, dst in zip(casts, cast_src, cast_dst):
        cast.emit(src, dst)

    u = jnp.dot(h_ref[...], w1_ref[...], preferred_element_type=F32)
    u = jnp.square(jnp.maximum(u, 0.0)).astype(BF16)
    out_ref[...] += jnp.dot(u, w2_ref[...], preferred_element_type=F32)


def _mlp(x, g, w1, w2, cast_weights=()):
    t = x.shape[0]
    n_i = t // MLP_TM
    casts = [_WeightCast(w, layer, n_i * MLP_NF, lambda i, f: i * MLP_NF + f, regroup=regroup)
             for w, layer, regroup in cast_weights]
    est = (4 * MLP_TM * D_MODEL * 4 + 4 * D_MODEL * MLP_TF * 2 + MLP_TM * D_MODEL * 2
           + 3 * MLP_TM * MLP_TF * 4 + 2 * MLP_TM * D_MODEL * 4
           + sum(c.vmem_bytes for c in casts))
    tile = lambda i, f: (i, 0)
    outs = pl.pallas_call(
        functools.partial(_mlp_body, casts=casts),
        out_shape=(jax.ShapeDtypeStruct((t, D_MODEL), F32), *[c.out_shape for c in casts]),
        grid=(n_i, MLP_NF),
        in_specs=[
            pl.BlockSpec((MLP_TM, D_MODEL), tile),
            pl.BlockSpec((1, D_MODEL), lambda i, f: (0, 0)),
            pl.BlockSpec((D_MODEL, MLP_TF), lambda i, f: (0, f)),
            pl.BlockSpec((MLP_TF, D_MODEL), lambda i, f: (f, 0)),
            *[c.in_spec for c in casts],
        ],
        out_specs=(pl.BlockSpec((MLP_TM, D_MODEL), tile), *[c.out_spec for c in casts]),
        scratch_shapes=[pltpu.VMEM((MLP_TM, D_MODEL), BF16)],
        compiler_params=pltpu.CompilerParams(
            dimension_semantics=("arbitrary", "arbitrary"), vmem_limit_bytes=_vmem_limit(est)),
        name="mlp",
    )(x, g, w1, w2, *[c.operand for c in casts])
    return outs[0], tuple(outs[1:])


def kernel(x, attn_norm, w_in, sgu_w, sgu_b, conv_w, q_norm, k_norm, w_out, mlp_norm, w_mlp_in,
           w_mlp_out):
    batch, seq, d_model = x.shape
    depth = w_in.shape[0]
    xt = x.reshape(batch * seq, d_model)

    head = lax.broadcasted_iota(jnp.int32, (GROUP_W, GROUP_W), 0) // HEAD_DIM
    ones_bd = (head == head.T).astype(BF16)
    perms = {dil: _deinterleave_matrix(dil) for dil in (4, 16)}
    unperms = {dil: perms[dil].T for dil in (4, 16)}

    w_in_b = _regroup_in_proj_cols(w_in[0]).astype(BF16)
    w1_b = w2_b = None

    for l in range(depth):
        sw = sgu_w[l].reshape(A_WIDTH // CHUNK, 2, CHUNK, CHUNK).transpose(0, 2, 1, 3)
        sw = sw.reshape(A_WIDTH // CHUNK, CHUNK, 2 * CHUNK).astype(BF16)
        sb = jnp.repeat(sgu_b[l].T, HEAD_DIM, axis=1)
        qg = jnp.tile(q_norm[l], 4)[None, :] * (HEAD_DIM ** -0.5 * LOG2_E)
        kg = jnp.tile(k_norm[l], 4)[None, :]

        ab, qkv0, qkv4, qkv16, w_out_b = _in_proj(xt, attn_norm[l][None, :], w_in_b, perms, qg, kg,
                                                  ones_bd, w_out, l)
        qkv0 = qkv0.reshape(batch * seq // PERM_BLK, 1, PERM_BLK, QKV_W)
        first = l == 0
        o_list, l_list, cast_out = _attention((qkv0, qkv4, qkv16), batch, seq,
                                              [(w_mlp_in, l)] if first else [])
        if first:
            (w1_b,) = cast_out
        xt, cast_out = _mix_out(ab, o_list, l_list, unperms, sw, sb, conv_w[l], w_out_b, xt, seq,
                                [(w_mlp_out, l)] if first else [])
        if first:
            (w2_b,) = cast_out
        nxt = l + 1
        xt, cast_out = _mlp(xt, mlp_norm[l][None, :], w1_b, w2_b,
                            [(w_in, nxt, True), (w_mlp_in, nxt, False), (w_mlp_out, nxt, False)]
                            if nxt < depth else [])
        if nxt < depth:
            w_in_b, w1_b, w2_b = cast_out
    return xt.reshape(batch, seq, d_model)
```

```python
import functools

import jax
import jax.numpy as jnp
from jax import lax
from jax.experimental import pallas as pl
from jax.experimental.pallas import tpu as pltpu

F32 = jnp.float32
BF16 = jnp.bfloat16

D_MODEL = 2048
HEAD_DIM = 64
A_WIDTH = 512
B_WIDTH = 768
CHUNK = 128
DILATIONS = (1, 4, 16)
ATT_BLK = 128
GROUP_W = 256
QKV_W = 3 * GROUP_W
PERM_BLK = 256
D_IN_PROJ = 5632
AB_WIDTH = 2 * A_WIDTH + 3 * B_WIDTH
Q_COL, K_COL, V_COL = 3328, 4096, 4864
D_FF = 4 * D_MODEL
EPS = 1e-6
NEG_BIG = -1e30
LOG2_E = 1.4426950408889634

V7X_VMEM_BYTES = 64 * 1024 * 1024


def _vmem_limit(estimate_bytes):
    return int(min(V7X_VMEM_BYTES - 4 * 1024 * 1024, estimate_bytes))


def _rms_norm_bf16(x, g):
    ms = jnp.mean(x * x, axis=-1, keepdims=True)
    return ((x * lax.rsqrt(ms + EPS)) * g).astype(BF16)


def _deinterleave_matrix(dil):
    out = lax.broadcasted_iota(jnp.int32, (PERM_BLK, PERM_BLK), 0)
    src = lax.broadcasted_iota(jnp.int32, (PERM_BLK, PERM_BLK), 1)
    per = PERM_BLK // dil
    return (src == (out % per) * dil + out // per).astype(BF16)


def _regroup_in_proj_cols(w):
    parts = [w[..., :AB_WIDTH]]
    for g in range(len(DILATIONS)):
        parts += [w[..., base + g * GROUP_W:base + (g + 1) * GROUP_W]
                  for base in (Q_COL, K_COL, V_COL)]
    return jnp.concatenate(parts, axis=-1)


class _WeightCast:
    def __init__(self, stacked, layer, steps, step_index, regroup=False):
        rows, cols = stacked.shape[1] // steps, stacked.shape[2]
        self.operand = stacked
        self.regroup = regroup
        self.in_spec = pl.BlockSpec((None, rows, cols),
                                    lambda *ids: (layer, step_index(*ids), 0))
        self.out_spec = pl.BlockSpec((rows, cols), lambda *ids: (step_index(*ids), 0))
        self.out_shape = jax.ShapeDtypeStruct(stacked.shape[1:], BF16)
        self.vmem_bytes = 2 * rows * cols * (4 + 2)

    def emit(self, src_ref, dst_ref):
        w = src_ref[...]
        dst_ref[...] = (_regroup_in_proj_cols(w) if self.regroup else w).astype(BF16)


CAST_STEPS = 16


def _cast_in_proj_weight(stacked, layer):
    cast = _WeightCast(stacked, layer, CAST_STEPS, lambda i: i, regroup=True)
    return pl.pallas_call(
        cast.emit,
        out_shape=cast.out_shape,
        grid=(CAST_STEPS,),
        in_specs=[cast.in_spec],
        out_specs=cast.out_spec,
        compiler_params=pltpu.CompilerParams(dimension_semantics=("arbitrary",)),
        name="cast_w_in",
    )(stacked)


IN_TM = 512
IN_TN = 512
IN_TAIL = (AB_WIDTH // IN_TN) * IN_TN


def _qk_head_norm(c, ones_bd, qg, kg):
    rows = c.shape[0]
    q, k = c[:, :GROUP_W], c[:, GROUP_W:2 * GROUP_W]
    sq = jnp.concatenate([q * q, k * k], axis=0).astype(BF16)
    ssq = jnp.dot(sq, ones_bd, preferred_element_type=F32)
    rinv = lax.rsqrt(ssq * (1.0 / HEAD_DIM) + EPS)
    qn = (q * rinv[:rows]) * qg
    kn = (k * rinv[rows:]) * kg
    return jnp.concatenate([qn.astype(BF16), kn.astype(BF16), c[:, 2 * GROUP_W:].astype(BF16)],
                           axis=1)


def _in_proj_body(x_ref, g_ref, w_ref, p4_ref, p16_ref, qg_ref, kg_ref, ones_ref, cast_src,
                  ab_ref, q0_ref, q4_ref, q16_ref, cast_dst, h_ref, tail_ref, *, cast):
    cast.emit(cast_src, cast_dst)
    h_ref[...] = _rms_norm_bf16(x_ref[...], g_ref[...])
    for start in range(0, D_IN_PROJ, IN_TN):
        cols = slice(start, start + IN_TN)
        c = jnp.dot(h_ref[...], w_ref[:, cols], preferred_element_type=F32)
        if start < IN_TAIL:
            ab_ref[:, cols] = c.astype(BF16)
        else:
            tail_ref[:, start - IN_TAIL:start - IN_TAIL + IN_TN] = c
    ab_ref[:, IN_TAIL:] = tail_ref[:, :AB_WIDTH - IN_TAIL].astype(BF16)
    for g, (dil, out_ref) in enumerate(zip(DILATIONS, (q0_ref, q4_ref, q16_ref))):
        lo = AB_WIDTH - IN_TAIL + g * QKV_W
        c = _qk_head_norm(tail_ref[:, lo:lo + QKV_W], ones_ref[...], qg_ref[...], kg_ref[...])
        if dil == 1:
            out_ref[...] = c
            continue
        perm = (p4_ref if dil == 4 else p16_ref)[...]
        for blk in range(IN_TM // PERM_BLK):
            pc = jnp.dot(perm, c[blk * PERM_BLK:(blk + 1) * PERM_BLK],
                         preferred_element_type=F32).astype(BF16)
            out_ref[blk] = pc.reshape(dil, PERM_BLK // dil, QKV_W)


def _in_proj(x, g, w, perms, qg, kg, ones_bd, w_out_stacked, layer):
    t = x.shape[0]
    cast = _WeightCast(w_out_stacked, layer, t // IN_TM, lambda i: i)
    nblk = t // PERM_BLK
    tb = IN_TM // PERM_BLK
    est = (2 * IN_TM * D_MODEL * 4 + D_MODEL * D_IN_PROJ * 2 + 2 * IN_TM * D_IN_PROJ * 2
           + IN_TM * D_MODEL * 2 + IN_TM * (D_IN_PROJ - IN_TAIL) * 4 + 8 * IN_TM * QKV_W * 4
           + cast.vmem_bytes)
    const2 = lambda i: (0, 0)
    return pl.pallas_call(
        functools.partial(_in_proj_body, cast=cast),
        out_shape=(
            jax.ShapeDtypeStruct((t, AB_WIDTH), BF16),
            jax.ShapeDtypeStruct((t, QKV_W), BF16),
            jax.ShapeDtypeStruct((nblk, 4, PERM_BLK // 4, QKV_W), BF16),
            jax.ShapeDtypeStruct((nblk, 16, PERM_BLK // 16, QKV_W), BF16),
            cast.out_shape,
        ),
        grid=(t // IN_TM,),
        in_specs=[
            pl.BlockSpec((IN_TM, D_MODEL), lambda i: (i, 0)),
            pl.BlockSpec((1, D_MODEL), const2),
            pl.BlockSpec((D_MODEL, D_IN_PROJ), const2, pipeline_mode=pl.Buffered(1)),
            pl.BlockSpec((PERM_BLK, PERM_BLK), const2),
            pl.BlockSpec((PERM_BLK, PERM_BLK), const2),
            pl.BlockSpec((1, GROUP_W), const2),
            pl.BlockSpec((1, GROUP_W), const2),
            pl.BlockSpec((GROUP_W, GROUP_W), const2),
            cast.in_spec,
        ],
        out_specs=(
            pl.BlockSpec((IN_TM, AB_WIDTH), lambda i: (i, 0)),
            pl.BlockSpec((IN_TM, QKV_W), lambda i: (i, 0)),
            pl.BlockSpec((tb, 4, PERM_BLK // 4, QKV_W), lambda i: (i, 0, 0, 0)),
            pl.BlockSpec((tb, 16, PERM_BLK // 16, QKV_W), lambda i: (i, 0, 0, 0)),
            cast.out_spec,
        ),
        scratch_shapes=[pltpu.VMEM((IN_TM, D_MODEL), BF16),
                        pltpu.VMEM((IN_TM, D_IN_PROJ - IN_TAIL), F32)],
        compiler_params=pltpu.CompilerParams(
            dimension_semantics=("arbitrary",), vmem_limit_bytes=_vmem_limit(est)),
        name="in_proj",
    )(x, g, w, perms[4], perms[16], qg, kg, ones_bd, cast.operand)


ATT_QROWS = 2 * ATT_BLK


def _head_lane_masks():
    lane = lax.broadcasted_iota(jnp.int32, (1, GROUP_W), 1)
    return [(lane >= h * HEAD_DIM) & (lane < (h + 1) * HEAD_DIM) for h in range(4)]


def _attend_block(q_blk, k_blk, v_blk, bias, head_masks):
    zero = jnp.zeros_like(q_blk)
    q4 = jnp.concatenate([jnp.where(hm, q_blk, zero) for hm in head_masks], axis=0)
    s4 = lax.dot_general(q4, k_blk, (((1,), (1,)), ((), ())), preferred_element_type=F32)
    e_list, m_list, den_list = [], [], []
    for h in range(4):
        s = s4[h * ATT_BLK:(h + 1) * ATT_BLK] + bias
        m = jnp.max(s, axis=-1, keepdims=True)
        e = jnp.exp2(s - m)
        m_list.append(m)
        den_list.append(jnp.sum(e, axis=-1, keepdims=True))
        e_list.append(e.astype(BF16))
    o4 = jnp.dot(jnp.concatenate(e_list, axis=0), v_blk, preferred_element_type=F32)
    o = jnp.zeros((ATT_BLK, GROUP_W), F32)
    lse2 = jnp.zeros((ATT_BLK, GROUP_W), F32)
    for h, hm in enumerate(head_masks):
        inv = 1.0 / den_list[h]
        o = jnp.where(hm, o4[h * ATT_BLK:(h + 1) * ATT_BLK] * inv, o)
        lse2 = jnp.where(hm, m_list[h] + jnp.log2(den_list[h]), lse2)
    return o.astype(BF16), lse2


def _attn_body(*refs, casts):
    n_in = 5 * len(DILATIONS)
    ins, cast_src = refs[:n_in], refs[n_in:n_in + len(casts)]
    outs = refs[n_in + len(casts):]
    cast_dst = outs[2 * len(DILATIONS):]
    for cast, src, dst in zip(casts, cast_src, cast_dst):
        cast.emit(src, dst)
    s = pl.program_id(1)
    head_masks = _head_lane_masks()
    tile = lambda ref, rows: ref[...].reshape(rows, GROUP_W)
    qi = lax.broadcasted_iota(jnp.int32, (ATT_BLK, 2 * ATT_BLK), 0)
    kj = lax.broadcasted_iota(jnp.int32, (ATT_BLK, 2 * ATT_BLK), 1)
    band = (kj >= qi) & (kj <= qi + ATT_BLK)
    band_bias = jnp.where(band, 0.0, NEG_BIG)

    for g, dil in enumerate(DILATIONS):
        q_ref, kc_ref, kp_ref, vc_ref, vp_ref = ins[5 * g:5 * g + 5]
        o_ref, lse_ref = outs[2 * g:2 * g + 2]
        qn = tile(q_ref, ATT_QROWS)
        kcat = jnp.concatenate([tile(kp_ref, ATT_BLK), tile(kc_ref, ATT_QROWS)], axis=0)
        vcat = jnp.concatenate([tile(vp_ref, ATT_BLK), tile(vc_ref, ATT_QROWS)], axis=0)
        first_kmin = jnp.where(s // dil > 0, 0, ATT_BLK)
        first_bias = jnp.where(band & (kj >= first_kmin), 0.0, NEG_BIG)
        o_rows, lse_rows = [], []
        for jb in range(2):
            o, lse = _attend_block(qn[jb * ATT_BLK:(jb + 1) * ATT_BLK],
                                   kcat[jb * ATT_BLK:(jb + 2) * ATT_BLK],
                                   vcat[jb * ATT_BLK:(jb + 2) * ATT_BLK],
                                   first_bias if jb == 0 else band_bias, head_masks)
            o_rows.append(o)
            lse_rows.append(lse)
        o_ref[...] = jnp.concatenate(o_rows, axis=0).reshape(o_ref.shape)
        lse_ref[...] = jnp.concatenate(lse_rows, axis=0).reshape(lse_ref.shape)


def _attention(qkvs, batch, seq, cast_weights=()):
    steps = seq // ATT_QROWS
    casts = [_WeightCast(w, layer, batch * steps, lambda b, s: b * steps + s)
             for w, layer in cast_weights]
    operands, in_specs, out_specs, out_shapes = [], [], [], []
    for dil, qkv in zip(DILATIONS, qkvs):
        nblk = qkv.shape[0]
        per = PERM_BLK // dil
        units = steps // dil
        if dil == 1:
            half = qkv.reshape(2 * nblk, 1, ATT_BLK, QKV_W)
            half_blocks = 1
        else:
            half = qkv
            half_blocks = dil // 2
        half_rows = half.shape[2]

        def unit(b, s, dil=dil, units=units):
            return b * units + s // dil

        def cur(col, dil=dil, per=per, unit=unit):
            return pl.BlockSpec((dil, None, per, GROUP_W),
                                lambda b, s: (unit(b, s), s % dil, 0, col))

        def prev(col, dil=dil, half_blocks=half_blocks, half_rows=half_rows, unit=unit):
            return pl.BlockSpec((half_blocks, None, half_rows, GROUP_W),
                                lambda b, s: (jnp.maximum(2 * unit(b, s) - 1, 0), s % dil, 0, col))

        operands += [qkv, qkv, half, qkv, half]
        in_specs += [cur(0), cur(1), prev(1), cur(2), prev(2)]
        out_specs += [cur(0), cur(0)]
        out_shapes += [jax.ShapeDtypeStruct((nblk, dil, per, GROUP_W), BF16),
                       jax.ShapeDtypeStruct((nblk, dil, per, GROUP_W), F32)]
    n_att = len(out_shapes)
    outs = pl.pallas_call(
        functools.partial(_attn_body, casts=casts),
        out_shape=(*out_shapes, *[c.out_shape for c in casts]),
        grid=(batch, steps),
        in_specs=in_specs + [c.in_spec for c in casts],
        out_specs=(*out_specs, *[c.out_spec for c in casts]),
        compiler_params=pltpu.CompilerParams(dimension_semantics=("arbitrary", "arbitrary")),
        name="attn",
    )(*operands, *[c.operand for c in casts])
    return list(outs[0:n_att:2]), list(outs[1:n_att:2]), tuple(outs[n_att:])


MIX_TM = 512
MIX_TN = 512
HALO = 16


def _split3_bf16(v):
    hi = v.astype(BF16)
    r1 = v - hi.astype(F32)
    mid = r1.astype(BF16)
    lo = (r1 - mid.astype(F32)).astype(BF16)
    return hi, mid, lo


def _token_order(o_ref, l_ref, unperm_ref):
    rows = o_ref.shape[0] * PERM_BLK
    o = o_ref[...].reshape(rows, GROUP_W)
    lse = l_ref[...].reshape(rows, GROUP_W)
    if unperm_ref is None:
        return o.astype(F32), lse
    stacked = jnp.concatenate([o, *_split3_bf16(lse)], axis=1)
    unperm = unperm_ref[...]
    nat = jnp.concatenate(
        [jnp.dot(unperm, stacked[b * PERM_BLK:(b + 1) * PERM_BLK], preferred_element_type=F32)
         for b in range(rows // PERM_BLK)], axis=0)
    lse_nat = (nat[:, GROUP_W:2 * GROUP_W] + nat[:, 2 * GROUP_W:3 * GROUP_W]) + nat[:, 3 * GROUP_W:]
    return nat[:, :GROUP_W], lse_nat


def _mix_out_body(ab_ref, halo_ref, o0_ref, o1_ref, o2_ref, l0_ref, l1_ref, l2_ref, u4_ref, u16_ref,
                  sw_ref, sb_ref, cw_ref, wo_ref, x_ref, *rest, tiles_per_seq, casts):
    cast_src, out_ref = rest[:len(casts)], rest[len(casts)]
    cast_dst, y_ref = rest[len(casts) + 1:2 * len(casts) + 1], rest[2 * len(casts) + 1]
    for cast, src, dst in zip(casts, cast_src, cast_dst):
        cast.emit(src, dst)
    i = pl.program_id(0)

    lane = lax.broadcasted_iota(jnp.int32, (CHUNK, CHUNK), 1)
    row = lax.broadcasted_iota(jnp.int32, (CHUNK, 2 * CHUNK), 0)
    colm = lax.broadcasted_iota(jnp.int32, (CHUNK, 2 * CHUNK), 1) % CHUNK
    causal = colm <= row
    for pair in range(A_WIDTH // CHUNK):
        w_pair = jnp.where(causal, sw_ref[pair], jnp.zeros((), BF16))
        ucols = slice(pair * CHUNK, (pair + 1) * CHUNK)
        vcols = slice(A_WIDTH + pair * CHUNK, A_WIDTH + (pair + 1) * CHUNK)
        bias = sb_ref[:, ucols]
        for c in range(MIX_TM // CHUNK):
            rows = slice(c * CHUNK, (c + 1) * CHUNK)
            v = ab_ref[rows, vcols]
            zero = jnp.zeros_like(v)
            rhs = jnp.concatenate([jnp.where(lane < HEAD_DIM, v, zero),
                                   jnp.where(lane >= HEAD_DIM, v, zero)], axis=0)
            mixed = jnp.dot(w_pair, rhs, preferred_element_type=F32) + bias
            y_ref[rows, ucols] = (ab_ref[rows, ucols].astype(F32) * mixed).astype(BF16)

    b0 = 2 * A_WIDTH
    gate = ab_ref[:, b0:b0 + B_WIDTH].astype(F32)
    z = (ab_ref[:, b0 + B_WIDTH:b0 + 2 * B_WIDTH].astype(F32)
         * ab_ref[:, b0 + 2 * B_WIDTH:b0 + 3 * B_WIDTH].astype(F32))
    zh = (halo_ref[HALO - 8:, b0 + B_WIDTH:b0 + 2 * B_WIDTH].astype(F32)
          * halo_ref[HALO - 8:, b0 + 2 * B_WIDTH:b0 + 3 * B_WIDTH].astype(F32))
    zh = jnp.where(i % tiles_per_seq == 0, jnp.zeros_like(zh), zh)
    zext = jnp.concatenate([zh, z], axis=0)
    conv = (cw_ref[0:1, :] * zext[6:6 + MIX_TM]
            + cw_ref[1:2, :] * zext[7:7 + MIX_TM]
            + cw_ref[2:3, :] * z)
    y_ref[:, A_WIDTH:A_WIDTH + B_WIDTH] = (gate * conv).astype(BF16)

    o0, l0 = _token_order(o0_ref, l0_ref, None)
    o1, l1 = _token_order(o1_ref, l1_ref, u4_ref)
    o2, l2 = _token_order(o2_ref, l2_ref, u16_ref)
    m = jnp.maximum(jnp.maximum(l0, l1), l2)
    e0, e1, e2 = jnp.exp2(l0 - m), jnp.exp2(l1 - m), jnp.exp2(l2 - m)
    inv = 1.0 / (e0 + e1 + e2)
    c0 = A_WIDTH + B_WIDTH
    for g, (o, e) in enumerate(((o0, e0), (o1, e1), (o2, e2))):
        y_ref[:, c0 + g * GROUP_W:c0 + (g + 1) * GROUP_W] = (o * (e * inv)).astype(BF16)

    for c in range(D_MODEL // MIX_TN):
        cols = slice(c * MIX_TN, (c + 1) * MIX_TN)
        out_ref[:, cols] = x_ref[:, cols] + jnp.dot(y_ref[...], wo_ref[:, cols],
                                                    preferred_element_type=F32)


def _mix_out(ab, o_list, l_list, unperms, sw, sb, cw, wo, x, seq, cast_weights=()):
    t = x.shape[0]
    casts = [_WeightCast(w, layer, t // MIX_TM, lambda i: i) for w, layer in cast_weights]
    tiles_per_seq = seq // MIX_TM
    halo_blocks = MIX_TM // HALO
    tb = MIX_TM // PERM_BLK
    row = lambda i: (i, 0)
    const2 = lambda i: (0, 0)

    def tile_spec(dil):
        return pl.BlockSpec((tb, dil, PERM_BLK // dil, GROUP_W), lambda i: (i, 0, 0, 0))

    est = (2 * MIX_TM * AB_WIDTH * 2 + 6 * MIX_TM * GROUP_W * 2 + 6 * MIX_TM * GROUP_W * 4
           + D_MODEL * D_MODEL * 2 + 4 * MIX_TM * D_MODEL * 4 + MIX_TM * D_MODEL * 2
           + 24 * MIX_TM * B_WIDTH * 4 + sum(c.vmem_bytes for c in casts))
    outs = pl.pallas_call(
        functools.partial(_mix_out_body, tiles_per_seq=tiles_per_seq, casts=casts),
        out_shape=(jax.ShapeDtypeStruct((t, D_MODEL), F32), *[c.out_shape for c in casts]),
        grid=(t // MIX_TM,),
        in_specs=[
            pl.BlockSpec((MIX_TM, AB_WIDTH), row),
            pl.BlockSpec((HALO, AB_WIDTH), lambda i: (jnp.maximum(i * halo_blocks - 1, 0), 0)),
            tile_spec(1), tile_spec(4), tile_spec(16),
            tile_spec(1), tile_spec(4), tile_spec(16),
            pl.BlockSpec((PERM_BLK, PERM_BLK), const2),
            pl.BlockSpec((PERM_BLK, PERM_BLK), const2),
            pl.BlockSpec((A_WIDTH // CHUNK, CHUNK, 2 * CHUNK), lambda i: (0, 0, 0)),
            pl.BlockSpec((CHUNK, A_WIDTH), const2),
            pl.BlockSpec((3, B_WIDTH), const2),
            pl.BlockSpec((D_MODEL, D_MODEL), const2, pipeline_mode=pl.Buffered(1)),
            pl.BlockSpec((MIX_TM, D_MODEL), row),
            *[c.in_spec for c in casts],
        ],
        out_specs=(pl.BlockSpec((MIX_TM, D_MODEL), row), *[c.out_spec for c in casts]),
        scratch_shapes=[pltpu.VMEM((MIX_TM, D_MODEL), BF16)],
        compiler_params=pltpu.CompilerParams(
            dimension_semantics=("arbitrary",), vmem_limit_bytes=_vmem_limit(est)),
        name="mix_out",
    )(ab, ab, *o_list, *l_list, unperms[4], unperms[16], sw, sb, cw, wo, x,
      *[c.operand for c in casts])
    return outs[0], tuple(outs[1:])


MLP_TM = 512
MLP_SUB = 1024
MLP_TF_HOST = 1024
MLP_TF = 2048


def _mlp_body(*refs, casts, tf):
    x_ref, g_ref, w1_ref, w2_ref = refs[:4]
    cast_src = refs[4:4 + len(casts)]
    out_ref = refs[4 + len(casts)]
    cast_dst = refs[5 + len(casts):5 + 2 * len(casts)]
    h_ref = refs[5 + 2 * len(casts)]

    @pl.when(pl.program_id(1) == 0)
    def _():
        h_ref[...] = _rms_norm_bf16(x_ref[...], g_ref[...])
        out_ref[...] = x_ref[...]

    for cast, src, dst in zip(casts, cast_src, cast_dst):
        cast.emit(src, dst)

    for c in range(tf // MLP_SUB):
        sub = slice(c * MLP_SUB, (c + 1) * MLP_SUB)
        u = jnp.dot(h_ref[...], w1_ref[:, sub], preferred_element_type=F32)
        u = jnp.square(jnp.maximum(u, 0.0)).astype(BF16)
        out_ref[...] += jnp.dot(u, w2_ref[sub, :], preferred_element_type=F32)


def _mlp(x, g, w1, w2, cast_weights=()):
    t = x.shape[0]
    n_i = t // MLP_TM
    tf = MLP_TF_HOST if cast_weights else MLP_TF
    n_f = D_FF // tf
    casts = [_WeightCast(w, layer, n_i * n_f, lambda i, f: i * n_f + f, regroup=regroup)
             for w, layer, regroup in cast_weights]
    est = (4 * MLP_TM * D_MODEL * 4 + 4 * D_MODEL * tf * 2 + MLP_TM * D_MODEL * 2
           + 3 * MLP_TM * MLP_SUB * 4 + 2 * MLP_TM * D_MODEL * 4
           + sum(c.vmem_bytes for c in casts))
    tile = lambda i, f: (i, 0)
    outs = pl.pallas_call(
        functools.partial(_mlp_body, casts=casts, tf=tf),
        out_shape=(jax.ShapeDtypeStruct((t, D_MODEL), F32), *[c.out_shape for c in casts]),
        grid=(n_i, n_f),
        in_specs=[
            pl.BlockSpec((MLP_TM, D_MODEL), tile),
            pl.BlockSpec((1, D_MODEL), lambda i, f: (0, 0)),
            pl.BlockSpec((D_MODEL, tf), lambda i, f: (0, f)),
            pl.BlockSpec((tf, D_MODEL), lambda i, f: (f, 0)),
            *[c.in_spec for c in casts],
        ],
        out_specs=(pl.BlockSpec((MLP_TM, D_MODEL), tile), *[c.out_spec for c in casts]),
        scratch_shapes=[pltpu.VMEM((MLP_TM, D_MODEL), BF16)],
        compiler_params=pltpu.CompilerParams(
            dimension_semantics=("arbitrary", "arbitrary"), vmem_limit_bytes=_vmem_limit(est)),
        name="mlp",
    )(x, g, w1, w2, *[c.operand for c in casts])
    return outs[0], tuple(outs[1:])


def kernel(x, attn_norm, w_in, sgu_w, sgu_b, conv_w, q_norm, k_norm, w_out, mlp_norm, w_mlp_in,
           w_mlp_out):
    batch, seq, d_model = x.shape
    depth = w_in.shape[0]
    xt = x.reshape(batch * seq, d_model)

    head = lax.broadcasted_iota(jnp.int32, (GROUP_W, GROUP_W), 0) // HEAD_DIM
    ones_bd = (head == head.T).astype(BF16)
    perms = {dil: _deinterleave_matrix(dil) for dil in (4, 16)}
    unperms = {dil: perms[dil].T for dil in (4, 16)}

    w_in_b = _cast_in_proj_weight(w_in, 0)
    w1_b = w2_b = None

    for l in range(depth):
        sw = sgu_w[l].reshape(A_WIDTH // CHUNK, 2, CHUNK, CHUNK).transpose(0, 2, 1, 3)
        sw = sw.reshape(A_WIDTH // CHUNK, CHUNK, 2 * CHUNK).astype(BF16)
        sb = jnp.repeat(sgu_b[l].T, HEAD_DIM, axis=1)
        qg = jnp.tile(q_norm[l], 4)[None, :] * (HEAD_DIM ** -0.5 * LOG2_E)
        kg = jnp.tile(k_norm[l], 4)[None, :]

        ab, qkv0, qkv4, qkv16, w_out_b = _in_proj(xt, attn_norm[l][None, :], w_in_b, perms, qg, kg,
                                                  ones_bd, w_out, l)
        qkv0 = qkv0.reshape(batch * seq // PERM_BLK, 1, PERM_BLK, QKV_W)
        first = l == 0
        o_list, l_list, cast_out = _attention((qkv0, qkv4, qkv16), batch, seq,
                                              [(w_mlp_in, l)] if first else [])
        if first:
            (w1_b,) = cast_out
        xt, cast_out = _mix_out(ab, o_list, l_list, unperms, sw, sb, conv_w[l], w_out_b, xt, seq,
                                [(w_mlp_out, l)] if first else [])
        if first:
            (w2_b,) = cast_out
        nxt = l + 1
        xt, cast_out = _mlp(xt, mlp_norm[l][None, :], w1_b, w2_b,
                            [(w_in, nxt, True), (w_mlp_in, nxt, False), (w_mlp_out, nxt, False)]
                            if nxt < depth else [])
        if nxt < depth:
            w_in_b, w1_b, w2_b = cast_out
    return xt.reshape(batch, seq, d_model)
```

```python
import functools

import jax
import jax.numpy as jnp
import numpy as np
from jax import lax
from jax.experimental import pallas as pl
from jax.experimental.pallas import tpu as pltpu

F32 = jnp.float32
BF16 = jnp.bfloat16

D_MODEL = 2048
HEAD_DIM = 64
A_WIDTH = 512
B_WIDTH = 768
CHUNK = 128
DILATIONS = (1, 4, 16)
ATT_BLK = 128
GROUP_W = 256
GROUP_HEADS = GROUP_W // HEAD_DIM
QKV_W = 3 * GROUP_W
PERM_BLK = 256
D_IN_PROJ = 5632
AB_WIDTH = 2 * A_WIDTH + 3 * B_WIDTH
MIX_IN_W = 2 * A_WIDTH + B_WIDTH
Q_COL, K_COL, V_COL = 3328, 4096, 4864
D_FF = 4 * D_MODEL
EPS = 1e-6
NEG_BIG = -1e30
LOG2_E = 1.4426950408889634

V7X_VMEM_BYTES = 64 * 1024 * 1024


def _vmem_limit(estimate_bytes):
    return int(min(V7X_VMEM_BYTES - 4 * 1024 * 1024, estimate_bytes))


NORM_ROWS = 64
NORM_COLS = 512


def _rms_norm_rows(x_ref, g_ref, h_ref, copy_ref=None):
    g = g_ref[...]
    d = x_ref.shape[1]
    for r in range(0, x_ref.shape[0], NORM_ROWS):
        rows = slice(r, r + NORM_ROWS)
        sq = jnp.zeros((NORM_ROWS, NORM_COLS), F32)
        for c in range(0, d, NORM_COLS):
            xc = x_ref[rows, c:c + NORM_COLS]
            sq = sq + xc * xc
        rinv = lax.rsqrt(jnp.sum(sq, axis=-1, keepdims=True) * (1.0 / d) + EPS)
        x = x_ref[rows, :]
        h_ref[rows, :] = ((x * rinv) * g).astype(BF16)
        if copy_ref is not None:
            copy_ref[rows, :] = x


def _deinterleave_matrix(dil):
    out = np.arange(PERM_BLK)[:, None]
    src = np.arange(PERM_BLK)[None, :]
    per = PERM_BLK // dil
    return (src == (out % per) * dil + out // per).astype(np.float32)


def _regroup_in_proj_cols(w):
    parts = [w[..., :AB_WIDTH]]
    for g in range(len(DILATIONS)):
        parts += [w[..., base + g * GROUP_W:base + (g + 1) * GROUP_W]
                  for base in (Q_COL, K_COL, V_COL)]
    return jnp.concatenate(parts, axis=-1)


class _WeightCast:
    def __init__(self, stacked, layer, steps, step_index, regroup=False):
        rows, cols = stacked.shape[1] // steps, stacked.shape[2]
        self.operand = stacked
        self.regroup = regroup
        self.in_spec = pl.BlockSpec((None, rows, cols),
                                    lambda *ids: (layer, step_index(*ids), 0))
        self.out_spec = pl.BlockSpec((rows, cols), lambda *ids: (step_index(*ids), 0))
        self.out_shape = jax.ShapeDtypeStruct(stacked.shape[1:], BF16)
        self.vmem_bytes = 2 * rows * cols * (4 + 2)

    def emit(self, src_ref, dst_ref):
        w = src_ref[...]
        dst_ref[...] = (_regroup_in_proj_cols(w) if self.regroup else w).astype(BF16)


CAST_STEPS = 8


def _cast_in_proj_weight(stacked, layer):
    cast = _WeightCast(stacked, layer, CAST_STEPS, lambda i: i, regroup=True)
    return pl.pallas_call(
        cast.emit,
        out_shape=cast.out_shape,
        grid=(CAST_STEPS,),
        in_specs=[cast.in_spec],
        out_specs=cast.out_spec,
        compiler_params=pltpu.CompilerParams(dimension_semantics=("arbitrary",)),
        name="cast_w_in",
    )(stacked)


IN_TM = 512
IN_TN = 512
IN_F32_START = 2 * A_WIDTH


def _qk_head_norm(c, ones_bd, qg, kg):
    rows = c.shape[0]
    q, k = c[:, :GROUP_W], c[:, GROUP_W:2 * GROUP_W]
    sq = jnp.concatenate([q * q, k * k], axis=0).astype(BF16)
    ssq = jnp.dot(sq, ones_bd, preferred_element_type=F32)
    rinv = lax.rsqrt(ssq * (1.0 / HEAD_DIM) + EPS)
    qn = (q * rinv[:rows]) * qg
    kn = (k * rinv[rows:]) * kg
    return jnp.concatenate([qn.astype(BF16), kn.astype(BF16), c[:, 2 * GROUP_W:].astype(BF16)],
                           axis=1)


def _in_proj_body(*refs, casts, tiles_per_seq):
    n = len(casts)
    x_ref, g_ref, w_ref, p4_ref, p16_ref, qg_ref, kg_ref, ones_ref, cw_ref = refs[:9]
    cast_src = refs[9:9 + n]
    mix_ref, q0_ref, q4_ref, q16_ref = refs[9 + n:13 + n]
    cast_dst = refs[13 + n:13 + 2 * n]
    h_ref, conv_ref, qkv_ref, zprev_ref = refs[13 + 2 * n:]

    @pl.when(pl.program_id(0) % tiles_per_seq == 0)
    def _():
        zprev_ref[...] = jnp.zeros_like(zprev_ref)

    for cast, src, dst in zip(casts, cast_src, cast_dst):
        cast.emit(src, dst)
    _rms_norm_rows(x_ref, g_ref, h_ref)
    for start in range(0, D_IN_PROJ, IN_TN):
        c = jnp.dot(h_ref[...], w_ref[:, start:start + IN_TN], preferred_element_type=F32)
        if start < IN_F32_START:
            mix_ref[:, start:start + IN_TN] = c.astype(BF16)
            continue
        n_conv = min(max(AB_WIDTH - start, 0), IN_TN)
        if n_conv:
            conv_ref[:, start - IN_F32_START:start - IN_F32_START + n_conv] = c[:, :n_conv]
        if n_conv < IN_TN:
            lo = start + n_conv - AB_WIDTH
            qkv_ref[:, lo:lo + IN_TN - n_conv] = c[:, n_conv:]

    gate = conv_ref[:, :B_WIDTH]
    z = conv_ref[:, B_WIDTH:2 * B_WIDTH] * conv_ref[:, 2 * B_WIDTH:]
    zext = jnp.concatenate([zprev_ref[...], z], axis=0)
    conv = (cw_ref[0:1, :] * zext[6:6 + IN_TM]
            + cw_ref[1:2, :] * zext[7:7 + IN_TM]
            + cw_ref[2:3, :] * z)
    mix_ref[:, IN_F32_START:] = (gate * conv).astype(BF16)
    zprev_ref[...] = z[IN_TM - 8:]

    for g, (dil, out_ref) in enumerate(zip(DILATIONS, (q0_ref, q4_ref, q16_ref))):
        c = _qk_head_norm(qkv_ref[:, g * QKV_W:(g + 1) * QKV_W], ones_ref[...], qg_ref[...],
                          kg_ref[...])
        if dil == 1:
            out_ref[...] = c
            continue
        perm = (p4_ref if dil == 4 else p16_ref)[...]
        for blk in range(IN_TM // PERM_BLK):
            pc = jnp.dot(perm, c[blk * PERM_BLK:(blk + 1) * PERM_BLK],
                         preferred_element_type=F32).astype(BF16)
            out_ref[blk] = pc.reshape(dil, PERM_BLK // dil, QKV_W)


def _in_proj(x, g, w, perms, qg, kg, ones_bd, cw, seq, cast_weights=()):
    t = x.shape[0]
    casts = [_WeightCast(w_, layer, t // IN_TM, lambda i: i, regroup=regroup)
             for w_, layer, regroup in cast_weights]
    nblk = t // PERM_BLK
    tb = IN_TM // PERM_BLK
    f32_cols = D_IN_PROJ - IN_F32_START
    est = (2 * IN_TM * D_MODEL * 4 + D_MODEL * D_IN_PROJ * 2
           + 2 * IN_TM * (MIX_IN_W + 3 * QKV_W) * 2 + IN_TM * D_MODEL * 2 + IN_TM * f32_cols * 4
           + 8 * IN_TM * QKV_W * 4 + sum(c.vmem_bytes for c in casts))
    const2 = lambda i: (0, 0)
    outs = pl.pallas_call(
        functools.partial(_in_proj_body, casts=casts, tiles_per_seq=seq // IN_TM),
        out_shape=(
            jax.ShapeDtypeStruct((t, MIX_IN_W), BF16),
            jax.ShapeDtypeStruct((t, QKV_W), BF16),
            jax.ShapeDtypeStruct((nblk, 4, PERM_BLK // 4, QKV_W), BF16),
            jax.ShapeDtypeStruct((nblk, 16, PERM_BLK // 16, QKV_W), BF16),
            *[c.out_shape for c in casts],
        ),
        grid=(t // IN_TM,),
        in_specs=[
            pl.BlockSpec((IN_TM, D_MODEL), lambda i: (i, 0)),
            pl.BlockSpec((1, D_MODEL), const2),
            pl.BlockSpec((D_MODEL, D_IN_PROJ), const2, pipeline_mode=pl.Buffered(1)),
            pl.BlockSpec((PERM_BLK, PERM_BLK), const2),
            pl.BlockSpec((PERM_BLK, PERM_BLK), const2),
            pl.BlockSpec((1, GROUP_W), const2),
            pl.BlockSpec((1, GROUP_W), const2),
            pl.BlockSpec((GROUP_W, GROUP_W), const2),
            pl.BlockSpec((3, B_WIDTH), const2),
            *[c.in_spec for c in casts],
        ],
        out_specs=(
            pl.BlockSpec((IN_TM, MIX_IN_W), lambda i: (i, 0)),
            pl.BlockSpec((IN_TM, QKV_W), lambda i: (i, 0)),
            pl.BlockSpec((tb, 4, PERM_BLK // 4, QKV_W), lambda i: (i, 0, 0, 0)),
            pl.BlockSpec((tb, 16, PERM_BLK // 16, QKV_W), lambda i: (i, 0, 0, 0)),
            *[c.out_spec for c in casts],
        ),
        scratch_shapes=[pltpu.VMEM((IN_TM, D_MODEL), BF16),
                        pltpu.VMEM((IN_TM, 3 * B_WIDTH), F32),
                        pltpu.VMEM((IN_TM, len(DILATIONS) * QKV_W), F32),
                        pltpu.VMEM((8, B_WIDTH), F32)],
        compiler_params=pltpu.CompilerParams(
            dimension_semantics=("arbitrary",), vmem_limit_bytes=_vmem_limit(est)),
        name="in_proj",
    )(x, g, w, perms[4], perms[16], qg, kg, ones_bd, cw, *[c.operand for c in casts])
    return outs[:4], tuple(outs[4:])


ATT_QROWS = 2 * ATT_BLK


def _head_lane_masks():
    lane = lax.broadcasted_iota(jnp.int32, (1, GROUP_W), 1)
    return [(lane >= h * HEAD_DIM) & (lane < (h + 1) * HEAD_DIM) for h in range(GROUP_HEADS)]


def _scores(q_blk, k_blk, head_masks):
    zero = jnp.zeros_like(q_blk)
    q4 = jnp.concatenate([jnp.where(hm, q_blk, zero) for hm in head_masks], axis=0)
    return lax.dot_general(k_blk, q4, (((1,), (1,)), ((), ())), preferred_element_type=F32)


def _softmax_cols(st, bias_t):
    e_list, m_list, den_list = [], [], []
    for h in range(GROUP_HEADS):
        s = st[:, h * ATT_BLK:(h + 1) * ATT_BLK] + bias_t
        m = jnp.max(s, axis=0, keepdims=True)
        e = jnp.exp2(s - m)
        m_list.append(m)
        den_list.append(jnp.sum(e, axis=0, keepdims=True))
        e_list.append(e.astype(BF16))
    return e_list, m_list, den_list


def _weighted_values(e_list, vt, m_list, den_list):
    o_rows, l_rows = [], []
    for h in range(GROUP_HEADS):
        ot = jnp.dot(vt[h * HEAD_DIM:(h + 1) * HEAD_DIM], e_list[h], preferred_element_type=F32)
        o_rows.append(ot * (1.0 / den_list[h]))
        l_rows.append(jnp.broadcast_to(m_list[h] + jnp.log2(den_list[h]), (HEAD_DIM, ATT_BLK)))
    o = jnp.concatenate(o_rows, axis=0).T
    lse2 = jnp.concatenate(l_rows, axis=0).T
    return o.astype(BF16), lse2


def _has_prev_block(dil, steps):
    return steps > dil


def _attn_body(*refs, casts, steps):
    n_in = sum(2 if _has_prev_block(dil, steps) else 1 for dil in DILATIONS)
    ins, cast_src = list(refs[:n_in]), refs[n_in:n_in + len(casts)]
    outs = refs[n_in + len(casts):]
    cast_dst = outs[2 * len(DILATIONS):]
    for cast, src, dst in zip(casts, cast_src, cast_dst):
        cast.emit(src, dst)
    s = pl.program_id(1)
    head_masks = _head_lane_masks()

    def part(ref, which, rows):
        return ref[:, :, which * GROUP_W:(which + 1) * GROUP_W].reshape(rows, GROUP_W)

    kj = lax.broadcasted_iota(jnp.int32, (2 * ATT_BLK, ATT_BLK), 0)
    qi = lax.broadcasted_iota(jnp.int32, (2 * ATT_BLK, ATT_BLK), 1)
    band = (kj >= qi) & (kj <= qi + ATT_BLK)
    band_bias = jnp.where(band, 0.0, NEG_BIG)
    causal = (lax.broadcasted_iota(jnp.int32, (ATT_BLK, ATT_BLK), 0)
              <= lax.broadcasted_iota(jnp.int32, (ATT_BLK, ATT_BLK), 1))
    causal_bias = jnp.where(causal, 0.0, NEG_BIG)

    problems = []
    for g, dil in enumerate(DILATIONS):
        cur_ref = ins.pop(0)
        qn = part(cur_ref, 0, ATT_QROWS)
        kc, vc = part(cur_ref, 1, ATT_QROWS), part(cur_ref, 2, ATT_QROWS)
        if _has_prev_block(dil, steps):
            prev_ref = ins.pop(0)
            kcat = jnp.concatenate([part(prev_ref, 1, ATT_BLK), kc], axis=0)
            vcat_t = jnp.concatenate([part(prev_ref, 2, ATT_BLK), vc], axis=0).T
            first_kmin = jnp.where(s // dil > 0, 0, ATT_BLK)
            first_bias = jnp.where(band & (kj >= first_kmin), 0.0, NEG_BIG)
            blocks = [(kcat[:2 * ATT_BLK], vcat_t[:, :2 * ATT_BLK], first_bias),
                      (kcat[ATT_BLK:], vcat_t[:, ATT_BLK:], band_bias)]
        else:
            vc_t = vc.T
            blocks = [(kc[:ATT_BLK], vc_t[:, :ATT_BLK], causal_bias), (kc, vc_t, band_bias)]
        for jb, (k_blk, vt_blk, bias) in enumerate(blocks):
            s4 = _scores(qn[jb * ATT_BLK:(jb + 1) * ATT_BLK], k_blk, head_masks)
            problems.append((s4, vt_blk, bias))
    stats = [_softmax_cols(st, bias) for st, _, bias in problems]
    results = [_weighted_values(e_list, vt_blk, m_list, den_list)
               for (_, vt_blk, _), (e_list, m_list, den_list) in zip(problems, stats)]
    for g in range(len(DILATIONS)):
        o_ref, lse_ref = outs[2 * g:2 * g + 2]
        (o0, l0), (o1, l1) = results[2 * g:2 * g + 2]
        o_ref[...] = jnp.concatenate([o0, o1], axis=0).reshape(o_ref.shape)
        lse_ref[...] = jnp.concatenate([l0, l1], axis=0).reshape(lse_ref.shape)


def _attention(qkvs, batch, seq, cast_weights=()):
    steps = seq // ATT_QROWS
    casts = [_WeightCast(w, layer, batch * steps, lambda b, s: b * steps + s)
             for w, layer in cast_weights]
    operands, in_specs, out_specs, out_shapes = [], [], [], []
    for dil, qkv in zip(DILATIONS, qkvs):
        nblk = qkv.shape[0]
        per = PERM_BLK // dil
        units = steps // dil

        def unit(b, s, dil=dil, units=units):
            return b * units + s // dil

        def cur(width, dil=dil, per=per, unit=unit):
            return pl.BlockSpec((dil, None, per, width), lambda b, s: (unit(b, s), s % dil, 0, 0))

        operands.append(qkv)
        in_specs.append(cur(QKV_W))
        if _has_prev_block(dil, steps):
            if dil == 1:
                half, half_blocks = qkv.reshape(2 * nblk, 1, ATT_BLK, QKV_W), 1
            else:
                half, half_blocks = qkv, dil // 2
            operands.append(half)
            in_specs.append(pl.BlockSpec(
                (half_blocks, None, half.shape[2], QKV_W),
                lambda b, s, dil=dil, unit=unit: (jnp.maximum(2 * unit(b, s) - 1, 0), s % dil,
                                                  0, 0)))
        out_specs += [cur(GROUP_W), cur(GROUP_W)]
        out_shapes += [jax.ShapeDtypeStruct((nblk, dil, per, GROUP_W), BF16),
                       jax.ShapeDtypeStruct((nblk, dil, per, GROUP_W), F32)]
    n_att = len(out_shapes)
    outs = pl.pallas_call(
        functools.partial(_attn_body, casts=casts, steps=steps),
        out_shape=(*out_shapes, *[c.out_shape for c in casts]),
        grid=(batch, steps),
        in_specs=in_specs + [c.in_spec for c in casts],
        out_specs=(*out_specs, *[c.out_spec for c in casts]),
        compiler_params=pltpu.CompilerParams(dimension_semantics=("arbitrary", "arbitrary")),
        name="attn",
    )(*operands, *[c.operand for c in casts])
    return list(outs[0:n_att:2]), list(outs[1:n_att:2]), tuple(outs[n_att:])


MIX_TM = 512
MIX_TN = 512


def _split3_bf16(v):
    hi = v.astype(BF16)
    r1 = v - hi.astype(F32)
    mid = r1.astype(BF16)
    lo = (r1 - mid.astype(F32)).astype(BF16)
    return hi, mid, lo


def _token_order(o_ref, l_ref, unperm_ref):
    rows = o_ref.shape[0] * PERM_BLK
    o = o_ref[...].reshape(rows, GROUP_W)
    lse = l_ref[...].reshape(rows, GROUP_W)
    if unperm_ref is None:
        return o.astype(F32), lse
    stacked = jnp.concatenate([o, *_split3_bf16(lse)], axis=1)
    unperm = unperm_ref[...]
    nat = jnp.concatenate(
        [jnp.dot(unperm, stacked[b * PERM_BLK:(b + 1) * PERM_BLK], preferred_element_type=F32)
         for b in range(rows // PERM_BLK)], axis=0)
    lse_nat = (nat[:, GROUP_W:2 * GROUP_W] + nat[:, 2 * GROUP_W:3 * GROUP_W]) + nat[:, 3 * GROUP_W:]
    return nat[:, :GROUP_W], lse_nat


def _mix_out_body(ab_ref, o0_ref, o1_ref, o2_ref, l0_ref, l1_ref, l2_ref, u4_ref, u16_ref,
                  sw_ref, sb_ref, wo_ref, x_ref, *rest, casts):
    cast_src, out_ref = rest[:len(casts)], rest[len(casts)]
    cast_dst, y_ref = rest[len(casts) + 1:2 * len(casts) + 1], rest[2 * len(casts) + 1]
    for cast, src, dst in zip(casts, cast_src, cast_dst):
        cast.emit(src, dst)

    lane = lax.broadcasted_iota(jnp.int32, (CHUNK, CHUNK), 1)
    row = lax.broadcasted_iota(jnp.int32, (CHUNK, 2 * CHUNK), 0)
    colm = lax.broadcasted_iota(jnp.int32, (CHUNK, 2 * CHUNK), 1) % CHUNK
    causal = colm <= row
    for pair in range(A_WIDTH // CHUNK):
        w_pair = jnp.where(causal, sw_ref[pair], jnp.zeros((), BF16))
        ucols = slice(pair * CHUNK, (pair + 1) * CHUNK)
        vcols = slice(A_WIDTH + pair * CHUNK, A_WIDTH + (pair + 1) * CHUNK)
        bias = sb_ref[:, ucols]
        for c in range(MIX_TM // CHUNK):
            rows = slice(c * CHUNK, (c + 1) * CHUNK)
            v = ab_ref[rows, vcols]
            zero = jnp.zeros_like(v)
            rhs = jnp.concatenate([jnp.where(lane < HEAD_DIM, v, zero),
                                   jnp.where(lane >= HEAD_DIM, v, zero)], axis=0)
            mixed = jnp.dot(w_pair, rhs, preferred_element_type=F32) + bias
            y_ref[rows, ucols] = (ab_ref[rows, ucols].astype(F32) * mixed).astype(BF16)

    y_ref[:, A_WIDTH:A_WIDTH + B_WIDTH] = ab_ref[:, 2 * A_WIDTH:]

    o0, l0 = _token_order(o0_ref, l0_ref, None)
    o1, l1 = _token_order(o1_ref, l1_ref, u4_ref)
    o2, l2 = _token_order(o2_ref, l2_ref, u16_ref)
    m = jnp.maximum(jnp.maximum(l0, l1), l2)
    e0, e1, e2 = jnp.exp2(l0 - m), jnp.exp2(l1 - m), jnp.exp2(l2 - m)
    inv = 1.0 / (e0 + e1 + e2)
    c0 = A_WIDTH + B_WIDTH
    for g, (o, e) in enumerate(((o0, e0), (o1, e1), (o2, e2))):
        y_ref[:, c0 + g * GROUP_W:c0 + (g + 1) * GROUP_W] = (o * (e * inv)).astype(BF16)

    for c in range(D_MODEL // MIX_TN):
        cols = slice(c * MIX_TN, (c + 1) * MIX_TN)
        out_ref[:, cols] = x_ref[:, cols] + jnp.dot(y_ref[...], wo_ref[:, cols],
                                                    preferred_element_type=F32)


def _mix_out(ab, o_list, l_list, unperms, sw, sb, wo, x, cast_weights=()):
    t = x.shape[0]
    casts = [_WeightCast(w, layer, t // MIX_TM, lambda i: i) for w, layer in cast_weights]
    tb = MIX_TM // PERM_BLK
    row = lambda i: (i, 0)
    const2 = lambda i: (0, 0)

    def tile_spec(dil):
        return pl.BlockSpec((tb, dil, PERM_BLK // dil, GROUP_W), lambda i: (i, 0, 0, 0))

    est = (2 * MIX_TM * MIX_IN_W * 2 + 6 * MIX_TM * GROUP_W * 2 + 6 * MIX_TM * GROUP_W * 4
           + D_MODEL * D_MODEL * 2 + 4 * MIX_TM * D_MODEL * 4 + MIX_TM * D_MODEL * 2
           + 24 * MIX_TM * B_WIDTH * 4 + sum(c.vmem_bytes for c in casts))
    outs = pl.pallas_call(
        functools.partial(_mix_out_body, casts=casts),
        out_shape=(jax.ShapeDtypeStruct((t, D_MODEL), F32), *[c.out_shape for c in casts]),
        grid=(t // MIX_TM,),
        in_specs=[
            pl.BlockSpec((MIX_TM, MIX_IN_W), row),
            tile_spec(1), tile_spec(4), tile_spec(16),
            tile_spec(1), tile_spec(4), tile_spec(16),
            pl.BlockSpec((PERM_BLK, PERM_BLK), const2),
            pl.BlockSpec((PERM_BLK, PERM_BLK), const2),
            pl.BlockSpec((A_WIDTH // CHUNK, CHUNK, 2 * CHUNK), lambda i: (0, 0, 0)),
            pl.BlockSpec((CHUNK, A_WIDTH), const2),
            pl.BlockSpec((D_MODEL, D_MODEL), const2, pipeline_mode=pl.Buffered(1)),
            pl.BlockSpec((MIX_TM, D_MODEL), row),
            *[c.in_spec for c in casts],
        ],
        out_specs=(pl.BlockSpec((MIX_TM, D_MODEL), row), *[c.out_spec for c in casts]),
        scratch_shapes=[pltpu.VMEM((MIX_TM, D_MODEL), BF16)],
        compiler_params=pltpu.CompilerParams(
            dimension_semantics=("arbitrary",), vmem_limit_bytes=_vmem_limit(est)),
        name="mix_out",
    )(ab, *o_list, *l_list, unperms[4], unperms[16], sw, sb, wo, x, *[c.operand for c in casts])
    return outs[0], tuple(outs[1:])


MLP_TM = 512
MLP_SUB = 1024
MLP_TF = 2048


def _mlp_body(*refs, casts, tf):
    x_ref, g_ref, w1_ref, w2_ref = refs[:4]
    cast_src = refs[4:4 + len(casts)]
    out_ref = refs[4 + len(casts)]
    cast_dst = refs[5 + len(casts):5 + 2 * len(casts)]
    h_ref = refs[5 + 2 * len(casts)]

    @pl.when(pl.program_id(1) == 0)
    def _():
        _rms_norm_rows(x_ref, g_ref, h_ref, copy_ref=out_ref)

    for cast, src, dst in zip(casts, cast_src, cast_dst):
        cast.emit(src, dst)

    for c in range(tf // MLP_SUB):
        sub = slice(c * MLP_SUB, (c + 1) * MLP_SUB)
        u = jnp.dot(h_ref[...], w1_ref[:, sub], preferred_element_type=F32)
        u = jnp.square(jnp.maximum(u, 0.0)).astype(BF16)
        out_ref[...] += jnp.dot(u, w2_ref[sub, :], preferred_element_type=F32)


def _mlp(x, g, w1, w2, cast_weights=()):
    t = x.shape[0]
    n_i = t // MLP_TM
    tf = MLP_TF
    n_f = D_FF // tf
    casts = [_WeightCast(w, layer, n_i * n_f, lambda i, f: i * n_f + f, regroup=regroup)
             for w, layer, regroup in cast_weights]
    est = (4 * MLP_TM * D_MODEL * 4 + 4 * D_MODEL * tf * 2 + MLP_TM * D_MODEL * 2
           + 3 * MLP_TM * MLP_SUB * 4 + 2 * MLP_TM * D_MODEL * 4
           + sum(c.vmem_bytes for c in casts))
    tile = lambda i, f: (i, 0)
    outs = pl.pallas_call(
        functools.partial(_mlp_body, casts=casts, tf=tf),
        out_shape=(jax.ShapeDtypeStruct((t, D_MODEL), F32), *[c.out_shape for c in casts]),
        grid=(n_i, n_f),
        in_specs=[
            pl.BlockSpec((MLP_TM, D_MODEL), tile),
            pl.BlockSpec((1, D_MODEL), lambda i, f: (0, 0)),
            pl.BlockSpec((D_MODEL, tf), lambda i, f: (0, f)),
            pl.BlockSpec((tf, D_MODEL), lambda i, f: (f, 0)),
            *[c.in_spec for c in casts],
        ],
        out_specs=(pl.BlockSpec((MLP_TM, D_MODEL), tile), *[c.out_spec for c in casts]),
        scratch_shapes=[pltpu.VMEM((MLP_TM, D_MODEL), BF16)],
        compiler_params=pltpu.CompilerParams(
            dimension_semantics=("arbitrary", "arbitrary"), vmem_limit_bytes=_vmem_limit(est)),
        name="mlp",
    )(x, g, w1, w2, *[c.operand for c in casts])
    return outs[0], tuple(outs[1:])


def kernel(x, attn_norm, w_in, sgu_w, sgu_b, conv_w, q_norm, k_norm, w_out, mlp_norm, w_mlp_in,
           w_mlp_out):
    batch, seq, d_model = x.shape
    depth = w_in.shape[0]
    xt = x.reshape(batch * seq, d_model)

    head = np.arange(GROUP_W) // HEAD_DIM
    ones_bd = jnp.asarray(head[:, None] == head[None, :], BF16)
    perms = {dil: jnp.asarray(_deinterleave_matrix(dil), BF16) for dil in (4, 16)}
    unperms = {dil: jnp.asarray(_deinterleave_matrix(dil).T, BF16) for dil in (4, 16)}

    w_in_b = _cast_in_proj_weight(w_in, 0)
    w1_b = w2_b = None

    for l in range(depth):
        sw = sgu_w[l].reshape(A_WIDTH // CHUNK, 2, CHUNK, CHUNK).transpose(0, 2, 1, 3)
        sw = sw.reshape(A_WIDTH // CHUNK, CHUNK, 2 * CHUNK).astype(BF16)
        sb = jnp.repeat(sgu_b[l].T, HEAD_DIM, axis=1)
        qg = jnp.tile(q_norm[l], GROUP_HEADS)[None, :] * (HEAD_DIM ** -0.5 * LOG2_E)
        kg = jnp.tile(k_norm[l], GROUP_HEADS)[None, :]

        first, nxt = l == 0, l + 1
        last = nxt == depth
        (ab, qkv0, qkv4, qkv16), cast_out = _in_proj(
            xt, attn_norm[l][None, :], w_in_b, perms, qg, kg, ones_bd, conv_w[l], seq,
            [(w_out, l, False)] + ([] if last else [(w_in, nxt, True)]))
        w_out_b = cast_out[0]
        if not last:
            w_in_b = cast_out[1]
        qkv0 = qkv0.reshape(batch * seq // PERM_BLK, 1, PERM_BLK, QKV_W)
        o_list, l_list, cast_out = _attention((qkv0, qkv4, qkv16), batch, seq,
                                              [(w_mlp_in, l)] if first else [])
        if first:
            (w1_b,) = cast_out
        xt, cast_out = _mix_out(ab, o_list, l_list, unperms, sw, sb, w_out_b, xt,
                                [(w_mlp_out if first else w_mlp_in, l)])
        if first:
            (w2_b,) = cast_out
        else:
            (w1_b,) = cast_out
        xt, cast_out = _mlp(xt, mlp_norm[l][None, :], w1_b, w2_b,
                            [] if last else [(w_mlp_out, nxt, False)])
        if not last:
            (w2_b,) = cast_out
    return xt.reshape(batch, seq, d_model)
```
